```python
import math
import jax, jax.numpy as jnp
from jax import lax
import numpy as np

D_MODEL = 2048
BATCH = 8
SEQ = 4096
DEPTH = 1
DEC_BATCH = 1
DEC_SEQ = 16384
PAST_LEN = 128

CONV_CH = D_MODEL // 2
CONV_WIDTH = 31
CONV_PAD = CONV_WIDTH // 2
N_HEADS = 8
HEAD_DK = 64
HEAD_DV = 2 * HEAD_DK
ATTN_QK = N_HEADS * 2 * HEAD_DK
ATTN_V = N_HEADS * HEAD_DV
MIX_WIDTH = CONV_CH + ATTN_V
IN_COLS = 2 * CONV_CH + 2 * ATTN_QK + ATTN_V
D_FF = 5632
N_BUCKETS = 32
MAX_DISTANCE = 128
Q_BLOCK = 128
RMS_EPS = 1e-6
LN_EPS = 1e-5
SUBLN_EPS = 1e-5

kernel_name = "hymba_conformer_diffattn_encoder"


def rms_norm(x, g, eps=RMS_EPS):
    xf = x.astype(jnp.float32)
    y = xf * lax.rsqrt(jnp.mean(xf * xf, axis=-1, keepdims=True) + eps)
    return (y * g.astype(jnp.float32)).astype(x.dtype)


def layer_norm(x, g, b, eps=LN_EPS):
    xf = x.astype(jnp.float32)
    mu = jnp.mean(xf, axis=-1, keepdims=True)
    xc = xf - mu
    var = jnp.mean(xc * xc, axis=-1, keepdims=True)
    y = xc * lax.rsqrt(var + eps) * g.astype(jnp.float32) + b.astype(jnp.float32)
    return y.astype(x.dtype)


def swiglu(x, w_in, w_out):
    gate, up = jnp.split(x @ w_in, 2, axis=-1)
    return (jax.nn.silu(gate) * up) @ w_out


def lambda_init_fn(layer):
    return 0.8 - 0.6 * math.exp(-0.3 * layer)


def rel_bucket(rel):
    half = N_BUCKETS // 2
    max_exact = half // 2
    ret = (rel > 0).astype(jnp.int32) * half
    n = jnp.abs(rel)
    nf = jnp.maximum(n, 1).astype(jnp.float32)
    large = max_exact + (jnp.log(nf / max_exact) / math.log(MAX_DISTANCE / max_exact)
                         * (half - max_exact)).astype(jnp.int32)
    large = jnp.minimum(large, half - 1)
    return ret + jnp.where(n < max_exact, n, large)


def conv_module(a, gate, conv_w, conv_b, ln_g, ln_b):
    u = a * jax.nn.sigmoid(gate)
    y = lax.conv_general_dilated(
        u, conv_w[:, None, :].astype(u.dtype),
        window_strides=(1,), padding=[(CONV_PAD, CONV_PAD)],
        dimension_numbers=('NWC', 'WIO', 'NWC'),
        feature_group_count=CONV_CH)
    y = y + conv_b.astype(y.dtype)
    return jax.nn.silu(layer_norm(y, ln_g, ln_b))


def diff_attention(q1, q2, k1, k2, v, lam, rel_bias):
    B, H, S, _ = q1.shape
    scale = HEAD_DK ** -0.5
    kpos = jnp.arange(S, dtype=jnp.int32)
    table = rel_bias.astype(jnp.float32)

    def one_block(i):
        start = i * Q_BLOCK
        qb1 = lax.dynamic_slice_in_dim(q1, start, Q_BLOCK, axis=2)
        qb2 = lax.dynamic_slice_in_dim(q2, start, Q_BLOCK, axis=2)
        qpos = start + jnp.arange(Q_BLOCK, dtype=jnp.int32)
        bucket = rel_bucket(kpos[None, :] - qpos[:, None])
        bias = jnp.transpose(table[bucket], (2, 0, 1))
        s1 = jnp.einsum('bhqd,bhkd->bhqk', qb1, k1).astype(jnp.float32) * scale + bias
        s2 = jnp.einsum('bhqd,bhkd->bhqk', qb2, k2).astype(jnp.float32) * scale + bias
        attn = jax.nn.softmax(s1, axis=-1) - lam * jax.nn.softmax(s2, axis=-1)
        return jnp.einsum('bhqk,bhkd->bhqd', attn.astype(v.dtype), v)

    out = lax.map(one_block, jnp.arange(S // Q_BLOCK))
    return jnp.transpose(out, (1, 2, 0, 3, 4)).reshape(B, H, S, HEAD_DV)


def mixer(h_in, layer, w_in, conv_w, conv_b, conv_ln_g, conv_ln_b,
          lambda_q1, lambda_k1, lambda_q2, lambda_k2, subln_g, w_out, rel_bias):
    B, S, _ = h_in.shape
    h = h_in @ w_in
    o = 0
    a = h[..., o:o + CONV_CH]; o += CONV_CH
    g = h[..., o:o + CONV_CH]; o += CONV_CH
    q = h[..., o:o + ATTN_QK].reshape(B, S, N_HEADS, 2, HEAD_DK); o += ATTN_QK
    k = h[..., o:o + ATTN_QK].reshape(B, S, N_HEADS, 2, HEAD_DK); o += ATTN_QK
    v = h[..., o:o + ATTN_V].reshape(B, S, N_HEADS, HEAD_DV)

    conv_out = conv_module(a, g, conv_w, conv_b, conv_ln_g, conv_ln_b)

    lam_init = lambda_init_fn(layer)
    lam = (jnp.exp(jnp.sum(lambda_q1.astype(jnp.float32) * lambda_k1.astype(jnp.float32)))
           - jnp.exp(jnp.sum(lambda_q2.astype(jnp.float32) * lambda_k2.astype(jnp.float32)))
           + lam_init)
    to_bhsd = lambda t: jnp.transpose(t, (0, 2, 1, 3))
    att = diff_attention(to_bhsd(q[..., 0, :]), to_bhsd(q[..., 1, :]),
                         to_bhsd(k[..., 0, :]), to_bhsd(k[..., 1, :]),
                         to_bhsd(v), lam, rel_bias)
    att = rms_norm(att, subln_g, SUBLN_EPS) * (1.0 - lam_init)
    att = jnp.transpose(att, (0, 2, 1, 3)).reshape(B, S, ATTN_V)

    return jnp.concatenate([conv_out, att.astype(conv_out.dtype)], axis=-1) @ w_out


def run_trunk(x, rel_bias, ffn1_norm, ffn1_w_in, ffn1_w_out, mix_norm, w_in,
              conv_w, conv_b, conv_ln_g, conv_ln_b, lambda_q1, lambda_k1,
              lambda_q2, lambda_k2, subln_g, w_out, ffn2_norm, ffn2_w_in,
              ffn2_w_out, final_norm):
    for l in range(DEPTH):
        x = x + 0.5 * swiglu(rms_norm(x, ffn1_norm[l]), ffn1_w_in[l], ffn1_w_out[l])
        x = x + mixer(rms_norm(x, mix_norm[l]), l, w_in[l], conv_w[l], conv_b[l],
                      conv_ln_g[l], conv_ln_b[l], lambda_q1[l], lambda_k1[l],
                      lambda_q2[l], lambda_k2[l], subln_g[l], w_out[l], rel_bias)
        x = x + 0.5 * swiglu(rms_norm(x, ffn2_norm[l]), ffn2_w_in[l], ffn2_w_out[l])
    return rms_norm(x, final_norm)


def setup_inputs(seed: int = 0) -> dict:
    key = jax.random.key(seed)
    ks = jax.random.split(key, 24)
    f32 = jnp.float32
    nrm = lambda k, shape, s: jax.random.normal(k, shape, f32) * s
    gain = lambda k, shape: 1.0 + 0.02 * jax.random.normal(k, shape, f32)
    return {
        "x_prompt": jax.random.normal(ks[0], (BATCH, SEQ, D_MODEL), f32),
        "x_sample": jax.random.normal(ks[1], (DEC_BATCH, DEC_SEQ, D_MODEL), f32),
        "rel_bias": nrm(ks[2], (N_BUCKETS, N_HEADS), 0.5),
        "ffn1_norm": gain(ks[3], (DEPTH, D_MODEL)),
        "ffn1_w_in": nrm(ks[4], (DEPTH, D_MODEL, 2 * D_FF), D_MODEL ** -0.5),
        "ffn1_w_out": nrm(ks[5], (DEPTH, D_FF, D_MODEL), D_FF ** -0.5),
        "mix_norm": gain(ks[6], (DEPTH, D_MODEL)),
        "w_in": nrm(ks[7], (DEPTH, D_MODEL, IN_COLS), D_MODEL ** -0.5),
        "conv_w": nrm(ks[8], (DEPTH, CONV_WIDTH, CONV_CH), CONV_WIDTH ** -0.5),
        "conv_b": nrm(ks[9], (DEPTH, CONV_CH), 0.02),
        "conv_ln_g": gain(ks[10], (DEPTH, CONV_CH)),
        "conv_ln_b": nrm(ks[11], (DEPTH, CONV_CH), 0.02),
        "lambda_q1": nrm(ks[12], (DEPTH, HEAD_DK), 0.1),
        "lambda_k1": nrm(ks[13], (DEPTH, HEAD_DK), 0.1),
        "lambda_q2": nrm(ks[14], (DEPTH, HEAD_DK), 0.1),
        "lambda_k2": nrm(ks[15], (DEPTH, HEAD_DK), 0.1),
        "subln_g": gain(ks[16], (DEPTH, HEAD_DV)),
        "w_out": nrm(ks[17], (DEPTH, MIX_WIDTH, D_MODEL), MIX_WIDTH ** -0.5),
        "ffn2_norm": gain(ks[18], (DEPTH, D_MODEL)),
        "ffn2_w_in": nrm(ks[19], (DEPTH, D_MODEL, 2 * D_FF), D_MODEL ** -0.5),
        "ffn2_w_out": nrm(ks[20], (DEPTH, D_FF, D_MODEL), D_FF ** -0.5),
        "final_norm": gain(ks[21], (D_MODEL,)),
    }


def reference(x_prompt, x_sample, rel_bias, ffn1_norm, ffn1_w_in, ffn1_w_out, mix_norm,
              w_in, conv_w, conv_b, conv_ln_g, conv_ln_b, lambda_q1, lambda_k1,
              lambda_q2, lambda_k2, subln_g, w_out, ffn2_norm, ffn2_w_in, ffn2_w_out,
              final_norm):
    y_prompt = run_trunk(x_prompt, rel_bias, ffn1_norm, ffn1_w_in, ffn1_w_out, mix_norm,
                         w_in, conv_w, conv_b, conv_ln_g, conv_ln_b, lambda_q1, lambda_k1,
                         lambda_q2, lambda_k2, subln_g, w_out, ffn2_norm, ffn2_w_in,
                         ffn2_w_out, final_norm)
    y_sample = run_trunk(x_sample, rel_bias, ffn1_norm, ffn1_w_in, ffn1_w_out, mix_norm,
                         w_in, conv_w, conv_b, conv_ln_g, conv_ln_b, lambda_q1, lambda_k1,
                         lambda_q2, lambda_k2, subln_g, w_out, ffn2_norm, ffn2_w_in,
                         ffn2_w_out, final_norm)
    return (y_prompt, y_sample)
```

```python
import functools
import math

import jax
import jax.numpy as jnp
from jax import lax
from jax.experimental import pallas as pl
from jax.experimental.pallas import tpu as pltpu

F32 = jnp.float32
BF16 = jnp.bfloat16

D_MODEL = 2048
CONV_CH = D_MODEL // 2
CONV_WIDTH = 31
CONV_PAD = CONV_WIDTH // 2
N_HEADS = 8
HEAD_DK = 64
HEAD_DV = 2 * HEAD_DK
ATTN_QK = N_HEADS * 2 * HEAD_DK
ATTN_V = N_HEADS * HEAD_DV
D_FF = 5632
N_BUCKETS = 32
MAX_DISTANCE = 128
RMS_EPS = 1e-6
LN_EPS = 1e-5
SUBLN_EPS = 1e-5
LAYER = 0
LAMBDA_INIT = 0.8 - 0.6 * math.exp(-0.3 * LAYER)

LANES = 128
SUBLANES = 8
VMEM_LIMIT_BYTES = 56 * 1024 * 1024

FFN_TM = 512
FFN_TF = 512
PROJ_TM = 512
CONV_TS = 512
CONV_HALO = 16
CONV_ROWS = 32
ATTN_T = 512
BIAS_SPAN = 2


def _rms(x, g, eps):
    return x * lax.rsqrt(jnp.mean(x * x, axis=-1, keepdims=True) + eps) * g


def _ffn_kernel(x_ref, g_ref, wg_ref, wu_ref, wo_ref, *rest, n_steps, final):
    if final:
        fg_ref, o_ref, xn_ref, acc_ref = rest
    else:
        o_ref, xn_ref, acc_ref = rest
    j = pl.program_id(1)

    @pl.when(j == 0)
    def _():
        xn_ref[...] = _rms(x_ref[...], g_ref[...], RMS_EPS).astype(BF16)
        acc_ref[...] = jnp.zeros_like(acc_ref)

    xn = xn_ref[...]
    gate = jnp.dot(xn, wg_ref[...], preferred_element_type=F32)
    up = jnp.dot(xn, wu_ref[...], preferred_element_type=F32)
    act = (gate * jax.nn.sigmoid(gate) * up).astype(BF16)
    acc_ref[...] += jnp.dot(act, wo_ref[...], preferred_element_type=F32)

    @pl.when(j == n_steps - 1)
    def _():
        y = x_ref[...] + 0.5 * acc_ref[...]
        if final:
            y = _rms(y, fg_ref[...], RMS_EPS)
        o_ref[...] = y


def _ffn(x, norm_g, w_in, w_out, final_g=None):
    m, d = x.shape
    tm, tf = FFN_TM, FFN_TF
    n_steps = D_FF // tf
    final = final_g is not None
    in_specs = [
        pl.BlockSpec((tm, d), lambda i, j: (i, 0)),
        pl.BlockSpec((1, d), lambda i, j: (0, 0)),
        pl.BlockSpec((d, tf), lambda i, j: (0, j)),
        pl.BlockSpec((d, tf), lambda i, j: (0, j + n_steps)),
        pl.BlockSpec((tf, d), lambda i, j: (j, 0)),
    ]
    args = [x, norm_g.reshape(1, d), w_in, w_in, w_out]
    if final:
        in_specs.append(pl.BlockSpec((1, d), lambda i, j: (0, 0)))
        args.append(final_g.reshape(1, d))
    return pl.pallas_call(
        functools.partial(_ffn_kernel, n_steps=n_steps, final=final),
        grid=(m // tm, n_steps),
        in_specs=in_specs,
        out_specs=pl.BlockSpec((tm, d), lambda i, j: (i, 0)),
        out_shape=jax.ShapeDtypeStruct((m, d), F32),
        scratch_shapes=[pltpu.VMEM((tm, d), BF16), pltpu.VMEM((tm, d), F32)],
        compiler_params=pltpu.CompilerParams(
            dimension_semantics=("arbitrary", "arbitrary"),
            vmem_limit_bytes=VMEM_LIMIT_BYTES),
        name="ffn_final" if final else "ffn",
    )(*args)


def _inproj_kernel(x_ref, g_ref, wa_ref, wgate_ref, wqkv_ref, u_ref, qkv_ref):
    xn = _rms(x_ref[...], g_ref[...], RMS_EPS).astype(BF16)
    a = jnp.dot(xn, wa_ref[...], preferred_element_type=F32)
    gate = jnp.dot(xn, wgate_ref[...], preferred_element_type=F32)
    u_ref[...] = a * jax.nn.sigmoid(gate)
    for c in range(3):
        cols = slice(c * ATTN_QK, (c + 1) * ATTN_QK)
        qkv_ref[:, cols] = jnp.dot(xn, wqkv_ref[:, cols], preferred_element_type=F32).astype(BF16)


def _inproj(x, norm_g, w_in):
    m, d = x.shape
    tm = PROJ_TM
    n_qkv = 2 * ATTN_QK + ATTN_V
    return pl.pallas_call(
        _inproj_kernel,
        grid=(m // tm,),
        in_specs=[
            pl.BlockSpec((tm, d), lambda i: (i, 0)),
            pl.BlockSpec((1, d), lambda i: (0, 0)),
            pl.BlockSpec((d, CONV_CH), lambda i: (0, 0)),
            pl.BlockSpec((d, CONV_CH), lambda i: (0, 1)),
            pl.BlockSpec((d, n_qkv), lambda i: (0, 0)),
        ],
        out_specs=[
            pl.BlockSpec((tm, CONV_CH), lambda i: (i, 0)),
            pl.BlockSpec((tm, n_qkv), lambda i: (i, 0)),
        ],
        out_shape=[
            jax.ShapeDtypeStruct((m, CONV_CH), F32),
            jax.ShapeDtypeStruct((m, n_qkv), BF16),
        ],
        compiler_params=pltpu.CompilerParams(
            dimension_semantics=("arbitrary",),
            vmem_limit_bytes=VMEM_LIMIT_BYTES),
        name="inproj",
    )(x, norm_g.reshape(1, d), w_in, w_in, w_in[:, 2 * CONV_CH:])


def _conv_kernel(prev_ref, cur_ref, next_ref, w_ref, cb_ref, lg_ref, lb_ref, o_ref, buf_ref,
                 *, ts, n_blocks):
    i = pl.program_id(1)
    halo = CONV_HALO
    prev = prev_ref[0]
    nxt = next_ref[0]
    buf_ref[0:halo, :] = jnp.where(i == 0, jnp.zeros_like(prev), prev)
    buf_ref[halo:halo + ts, :] = cur_ref[0]
    buf_ref[halo + ts:, :] = jnp.where(i == n_blocks - 1, jnp.zeros_like(nxt), nxt)

    rows = CONV_ROWS
    first = halo - CONV_PAD
    for c in range(ts // rows):
        r0 = c * rows
        acc = jnp.zeros((rows, CONV_CH), F32)
        for t in range(CONV_WIDTH):
            s = r0 + first + t
            acc = acc + buf_ref[s:s + rows, :] * w_ref[t:t + 1, :]
        y = acc + cb_ref[...]
        mu = jnp.mean(y, axis=-1, keepdims=True)
        yc = y - mu
        var = jnp.mean(yc * yc, axis=-1, keepdims=True)
        z = yc * lax.rsqrt(var + LN_EPS) * lg_ref[...] + lb_ref[...]
        o_ref[0, r0:r0 + rows, :] = (z * jax.nn.sigmoid(z)).astype(BF16)


def _conv(u, conv_w, conv_b, ln_g, ln_b):
    b, s, c = u.shape
    ts = CONV_TS
    n_blocks = s // ts
    per = ts // CONV_HALO
    n_halo = s // CONV_HALO
    row = lambda v: v.reshape(1, c)
    return pl.pallas_call(
        functools.partial(_conv_kernel, ts=ts, n_blocks=n_blocks),
        grid=(b, n_blocks),
        in_specs=[
            pl.BlockSpec((1, CONV_HALO, c), lambda bi, i: (bi, jnp.maximum(i * per - 1, 0), 0)),
            pl.BlockSpec((1, ts, c), lambda bi, i: (bi, i, 0)),
            pl.BlockSpec((1, CONV_HALO, c), lambda bi, i: (bi, jnp.minimum((i + 1) * per, n_halo - 1), 0)),
            pl.BlockSpec((CONV_WIDTH, c), lambda bi, i: (0, 0)),
            pl.BlockSpec((1, c), lambda bi, i: (0, 0)),
            pl.BlockSpec((1, c), lambda bi, i: (0, 0)),
            pl.BlockSpec((1, c), lambda bi, i: (0, 0)),
        ],
        out_specs=pl.BlockSpec((1, ts, c), lambda bi, i: (bi, i, 0)),
        out_shape=jax.ShapeDtypeStruct((b, s, c), BF16),
        scratch_shapes=[pltpu.VMEM((ts + 2 * CONV_HALO, c), F32)],
        compiler_params=pltpu.CompilerParams(
            dimension_semantics=("arbitrary", "arbitrary"),
            vmem_limit_bytes=VMEM_LIMIT_BYTES),
        name="conv",
    )(u, u, u, conv_w, row(conv_b), row(ln_g), row(ln_b))


def _attn_kernel(q_ref, k_ref, v_ref, bias_ref, lq1_ref, lk1_ref, lq2_ref, lk2_ref, g_ref, o_ref,
                 acc1_ref, acc2_ref, m1_ref, m2_ref, l1_ref, l2_ref, *, tile, n_k):
    i = pl.program_id(2)
    q = q_ref[0] * jnp.asarray(HEAD_DK ** -0.5, BF16)
    lane = lax.broadcasted_iota(jnp.int32, q.shape, 1)
    zero = jnp.zeros_like(q)
    q_maps = (jnp.where(lane < HEAD_DK, q, zero), jnp.where(lane >= HEAD_DK, q, zero))
    state = ((acc1_ref, m1_ref, l1_ref), (acc2_ref, m2_ref, l2_ref))
    for acc_ref, m_ref, l_ref in state:
        acc_ref[...] = jnp.zeros_like(acc_ref)
        m_ref[...] = jnp.full_like(m_ref, -jnp.inf)
        l_ref[...] = jnp.zeros_like(l_ref)

    def body(j, carry):
        start = pl.multiple_of(j * tile, tile)
        k = k_ref[0, pl.ds(start, tile), :]
        v = v_ref[0, pl.ds(start, tile), :]
        bias = bias_ref[0, jnp.clip(j - i, -BIAS_SPAN, BIAS_SPAN) + BIAS_SPAN]
        for qm, (acc_ref, m_ref, l_ref) in zip(q_maps, state):
            s = lax.dot_general(qm, k, (((1,), (1,)), ((), ())), preferred_element_type=F32) + bias
            m_prev = m_ref[...]
            m_new = jnp.maximum(m_prev, jnp.max(s, axis=1, keepdims=True))
            p = jnp.exp(s - m_new)
            alpha = jnp.exp(m_prev - m_new)
            l_ref[...] = alpha * l_ref[...] + jnp.sum(p, axis=1, keepdims=True)
            acc_ref[...] = alpha * acc_ref[...] + jnp.dot(p.astype(BF16), v, preferred_element_type=F32)
            m_ref[...] = m_new
        return carry

    lax.fori_loop(0, n_k, body, 0)

    lam = (jnp.exp(jnp.sum(lq1_ref[...] * lk1_ref[...], axis=-1, keepdims=True))
           - jnp.exp(jnp.sum(lq2_ref[...] * lk2_ref[...], axis=-1, keepdims=True))
           + LAMBDA_INIT)
    o = acc1_ref[...] / l1_ref[...] - lam * (acc2_ref[...] / l2_ref[...])
    o = _rms(o, g_ref[...], SUBLN_EPS) * (1.0 - LAMBDA_INIT)
    o_ref[0] = o.astype(BF16)


def _rel_bucket(rel):
    half = N_BUCKETS // 2
    max_exact = half // 2
    ret = (rel > 0).astype(jnp.int32) * half
    n = jnp.abs(rel)
    nf = jnp.maximum(n, 1).astype(jnp.float32)
    large = max_exact + (jnp.log(nf / max_exact) / math.log(MAX_DISTANCE / max_exact)
                         * (half - max_exact)).astype(jnp.int32)
    large = jnp.minimum(large, half - 1)
    return ret + jnp.where(n < max_exact, n, large)


def _bias_tiles(rel_bias, tile):
    assert tile >= MAX_DISTANCE
    lo = -(BIAS_SPAN + 1) * tile
    rel = jnp.arange(lo, -lo, dtype=jnp.int32)
    by_rel = rel_bias.astype(F32)[_rel_bucket(rel)].T
    tiles = []
    for d in range(-BIAS_SPAN, BIAS_SPAN + 1):
        pos = lax.slice_in_dim(by_rel, d * tile - lo, (d + 1) * tile - lo, axis=1)
        neg = lax.slice_in_dim(by_rel, (d - 1) * tile - lo, d * tile - lo, axis=1)
        z = jnp.concatenate([pos, neg], axis=1)
        skew = jnp.tile(z, (1, tile))[:, :tile * (2 * tile - 1)].reshape(-1, tile, 2 * tile - 1)
        tiles.append(skew[:, :, :tile])
    return jnp.stack(tiles, axis=1)


def _attn(qkv, bias_tiles, lq1, lk1, lq2, lk2, subln_g):
    b, s, _ = qkv.shape
    tile = ATTN_T
    n_k = s // tile
    n_bias = 2 * BIAS_SPAN + 1
    k_col = ATTN_QK // LANES
    v_col = 2 * ATTN_QK // LANES
    vec = lambda v: v.reshape(1, -1).astype(F32)
    small = lambda n: pl.BlockSpec((1, n), lambda bi, h, i: (0, 0))
    return pl.pallas_call(
        functools.partial(_attn_kernel, tile=tile, n_k=n_k),
        grid=(b, N_HEADS, s // tile),
        in_specs=[
            pl.BlockSpec((1, tile, LANES), lambda bi, h, i: (bi, i, h)),
            pl.BlockSpec((1, s, LANES), lambda bi, h, i: (bi, 0, k_col + h)),
            pl.BlockSpec((1, s, LANES), lambda bi, h, i: (bi, 0, v_col + h)),
            pl.BlockSpec((1, n_bias, tile, tile), lambda bi, h, i: (h, 0, 0, 0)),
            small(HEAD_DK), small(HEAD_DK), small(HEAD_DK), small(HEAD_DK), small(HEAD_DV),
        ],
        out_specs=pl.BlockSpec((1, tile, LANES), lambda bi, h, i: (bi, i, h)),
        out_shape=jax.ShapeDtypeStruct((b, s, ATTN_V), BF16),
        scratch_shapes=[
            pltpu.VMEM((tile, HEAD_DV), F32), pltpu.VMEM((tile, HEAD_DV), F32),
            pltpu.VMEM((tile, 1), F32), pltpu.VMEM((tile, 1), F32),
            pltpu.VMEM((tile, 1), F32), pltpu.VMEM((tile, 1), F32),
        ],
        compiler_params=pltpu.CompilerParams(
            dimension_semantics=("arbitrary", "arbitrary", "arbitrary"),
            vmem_limit_bytes=VMEM_LIMIT_BYTES),
        name="diff_attn",
    )(qkv, qkv, qkv, bias_tiles, vec(lq1), vec(lk1), vec(lq2), vec(lk2), vec(subln_g))


def _outproj_kernel(x_ref, c_ref, a_ref, wc_ref, wa_ref, o_ref):
    o_ref[...] = (x_ref[...]
                  + jnp.dot(c_ref[...], wc_ref[...], preferred_element_type=F32)
                  + jnp.dot(a_ref[...], wa_ref[...], preferred_element_type=F32))


def _outproj(x, conv_out, att, w_out):
    m, d = x.shape
    tm = PROJ_TM
    return pl.pallas_call(
        _outproj_kernel,
        grid=(m // tm,),
        in_specs=[
            pl.BlockSpec((tm, d), lambda i: (i, 0)),
            pl.BlockSpec((tm, CONV_CH), lambda i: (i, 0)),
            pl.BlockSpec((tm, ATTN_V), lambda i: (i, 0)),
            pl.BlockSpec((CONV_CH, d), lambda i: (0, 0)),
            pl.BlockSpec((ATTN_V, d), lambda i: (1, 0)),
        ],
        out_specs=pl.BlockSpec((tm, d), lambda i: (i, 0)),
        out_shape=jax.ShapeDtypeStruct((m, d), F32),
        compiler_params=pltpu.CompilerParams(
            dimension_semantics=("arbitrary",),
            vmem_limit_bytes=VMEM_LIMIT_BYTES),
        name="outproj",
    )(x, conv_out, att, w_out, w_out)


def _trunk(x, w, bias_tiles):
    b, s, d = x.shape
    m = b * s
    x0 = x.reshape(m, d)
    x1 = _ffn(x0, w["ffn1_norm"], w["ffn1_w_in"], w["ffn1_w_out"])
    u, qkv = _inproj(x1, w["mix_norm"], w["w_in"])
    conv_out = _conv(u.reshape(b, s, CONV_CH), w["conv_w"], w["conv_b"], w["conv_ln_g"], w["conv_ln_b"])
    att = _attn(qkv.reshape(b, s, -1), bias_tiles, w["lambda_q1"], w["lambda_k1"],
                w["lambda_q2"], w["lambda_k2"], w["subln_g"])
    x2 = _outproj(x1, conv_out.reshape(m, CONV_CH), att.reshape(m, ATTN_V), w["w_out"])
    y = _ffn(x2, w["ffn2_norm"], w["ffn2_w_in"], w["ffn2_w_out"], final_g=w["final_norm"])
    return y.reshape(b, s, d)


def kernel(x_prompt, x_sample, rel_bias, ffn1_norm, ffn1_w_in, ffn1_w_out, mix_norm, w_in, conv_w, conv_b, conv_ln_g, conv_ln_b, lambda_q1, lambda_k1, lambda_q2, lambda_k2, subln_g, w_out, ffn2_norm, ffn2_w_in, ffn2_w_out, final_norm):
    assert ffn1_norm.shape[0] == 1, "single-layer trunk"
    w = dict(
        ffn1_norm=ffn1_norm[0], ffn1_w_in=ffn1_w_in[0].astype(BF16), ffn1_w_out=ffn1_w_out[0].astype(BF16),
        mix_norm=mix_norm[0], w_in=w_in[0].astype(BF16),
        conv_w=conv_w[0], conv_b=conv_b[0], conv_ln_g=conv_ln_g[0], conv_ln_b=conv_ln_b[0],
        lambda_q1=lambda_q1[0], lambda_k1=lambda_k1[0], lambda_q2=lambda_q2[0], lambda_k2=lambda_k2[0],
        subln_g=subln_g[0], w_out=w_out[0].astype(BF16),
        ffn2_norm=ffn2_norm[0], ffn2_w_in=ffn2_w_in[0].astype(BF16), ffn2_w_out=ffn2_w_out[0].astype(BF16),
        final_norm=final_norm,
    )
    bias_tiles = _bias_tiles(rel_bias, ATTN_T)
    return (_trunk(x_prompt, w, bias_tiles), _trunk(x_sample, w, bias_tiles))
```

```python
import functools
import math

import jax
import jax.numpy as jnp
from jax import lax
from jax.experimental import pallas as pl
from jax.experimental.pallas import tpu as pltpu

F32 = jnp.float32
BF16 = jnp.bfloat16

D_MODEL = 2048
CONV_CH = D_MODEL // 2
CONV_WIDTH = 31
CONV_PAD = CONV_WIDTH // 2
N_HEADS = 8
HEAD_DK = 64
HEAD_DV = 2 * HEAD_DK
ATTN_QK = N_HEADS * 2 * HEAD_DK
ATTN_V = N_HEADS * HEAD_DV
D_FF = 5632
N_BUCKETS = 32
MAX_DISTANCE = 128
RMS_EPS = 1e-6
LN_EPS = 1e-5
SUBLN_EPS = 1e-5
LAYER = 0
LAMBDA_INIT = 0.8 - 0.6 * math.exp(-0.3 * LAYER)
LOG2E = math.log2(math.e)
Q_SCALE = HEAD_DK ** -0.5 * LOG2E

LANES = 128
SUBLANES = 8
VMEM_LIMIT_BYTES = 56 * 1024 * 1024

FFN_TM = 512
FFN_TF = 512
PROJ_TM = 512
CONV_TS = 512
CONV_HALO = 16
CONV_ROWS = 32
ATTN_T = 512
BIAS_SPAN = 2
N_NEAR = 2 * BIAS_SPAN - 1


def _rms(x, g, eps):
    return x * lax.rsqrt(jnp.mean(x * x, axis=-1, keepdims=True) + eps) * g


def _ffn_kernel(x_ref, g_ref, wg_ref, wu_ref, wo_ref, *rest, n_steps, final):
    if final:
        fg_ref, o_ref, xn_ref, acc_ref = rest
    else:
        o_ref, xn_ref, acc_ref = rest
    j = pl.program_id(1)

    @pl.when(j == 0)
    def _():
        xn_ref[...] = _rms(x_ref[...], g_ref[...], RMS_EPS).astype(BF16)
        acc_ref[...] = jnp.zeros_like(acc_ref)

    xn = xn_ref[...]
    gate = jnp.dot(xn, wg_ref[...], preferred_element_type=F32)
    up = jnp.dot(xn, wu_ref[...], preferred_element_type=F32)
    act = (gate * jax.nn.sigmoid(gate) * up).astype(BF16)
    acc_ref[...] += jnp.dot(act, wo_ref[...], preferred_element_type=F32)

    @pl.when(j == n_steps - 1)
    def _():
        y = x_ref[...] + 0.5 * acc_ref[...]
        if final:
            y = _rms(y, fg_ref[...], RMS_EPS)
        o_ref[...] = y


def _ffn(x, norm_g, w_in, w_out, final_g=None):
    m, d = x.shape
    tm, tf = FFN_TM, FFN_TF
    n_steps = D_FF // tf
    final = final_g is not None
    in_specs = [
        pl.BlockSpec((tm, d), lambda i, j: (i, 0)),
        pl.BlockSpec((1, d), lambda i, j: (0, 0)),
        pl.BlockSpec((d, tf), lambda i, j: (0, j)),
        pl.BlockSpec((d, tf), lambda i, j: (0, j + n_steps)),
        pl.BlockSpec((tf, d), lambda i, j: (j, 0)),
    ]
    args = [x, norm_g.reshape(1, d), w_in, w_in, w_out]
    if final:
        in_specs.append(pl.BlockSpec((1, d), lambda i, j: (0, 0)))
        args.append(final_g.reshape(1, d))
    return pl.pallas_call(
        functools.partial(_ffn_kernel, n_steps=n_steps, final=final),
        grid=(m // tm, n_steps),
        in_specs=in_specs,
        out_specs=pl.BlockSpec((tm, d), lambda i, j: (i, 0)),
        out_shape=jax.ShapeDtypeStruct((m, d), F32),
        scratch_shapes=[pltpu.VMEM((tm, d), BF16), pltpu.VMEM((tm, d), F32)],
        compiler_params=pltpu.CompilerParams(
            dimension_semantics=("arbitrary", "arbitrary"),
            vmem_limit_bytes=VMEM_LIMIT_BYTES),
        name="ffn_final" if final else "ffn",
    )(*args)


def _inproj_kernel(x_ref, g_ref, wa_ref, wgate_ref, wqkv_ref, u_ref, qkv_ref):
    xn = _rms(x_ref[...], g_ref[...], RMS_EPS).astype(BF16)
    a = jnp.dot(xn, wa_ref[...], preferred_element_type=F32)
    gate = jnp.dot(xn, wgate_ref[...], preferred_element_type=F32)
    u_ref[...] = a * jax.nn.sigmoid(gate)
    for c, scale in enumerate((Q_SCALE, None, None)):
        cols = slice(c * ATTN_QK, (c + 1) * ATTN_QK)
        h = jnp.dot(xn, wqkv_ref[:, cols], preferred_element_type=F32)
        qkv_ref[:, cols] = (h if scale is None else h * scale).astype(BF16)


def _inproj(x, norm_g, w_in):
    m, d = x.shape
    tm = PROJ_TM
    n_qkv = 2 * ATTN_QK + ATTN_V
    return pl.pallas_call(
        _inproj_kernel,
        grid=(m // tm,),
        in_specs=[
            pl.BlockSpec((tm, d), lambda i: (i, 0)),
            pl.BlockSpec((1, d), lambda i: (0, 0)),
            pl.BlockSpec((d, CONV_CH), lambda i: (0, 0)),
            pl.BlockSpec((d, CONV_CH), lambda i: (0, 1)),
            pl.BlockSpec((d, n_qkv), lambda i: (0, 0)),
        ],
        out_specs=[
            pl.BlockSpec((tm, CONV_CH), lambda i: (i, 0)),
            pl.BlockSpec((tm, n_qkv), lambda i: (i, 0)),
        ],
        out_shape=[
            jax.ShapeDtypeStruct((m, CONV_CH), F32),
            jax.ShapeDtypeStruct((m, n_qkv), BF16),
        ],
        compiler_params=pltpu.CompilerParams(
            dimension_semantics=("arbitrary",),
            vmem_limit_bytes=VMEM_LIMIT_BYTES),
        name="inproj",
    )(x, norm_g.reshape(1, d), w_in, w_in, w_in[:, 2 * CONV_CH:])


def _conv_kernel(prev_ref, cur_ref, next_ref, w_ref, cb_ref, lg_ref, lb_ref, o_ref, buf_ref,
                 *, ts, n_blocks):
    i = pl.program_id(1)
    halo = CONV_HALO
    prev = prev_ref[0]
    nxt = next_ref[0]
    buf_ref[0:halo, :] = jnp.where(i == 0, jnp.zeros_like(prev), prev)
    buf_ref[halo:halo + ts, :] = cur_ref[0]
    buf_ref[halo + ts:, :] = jnp.where(i == n_blocks - 1, jnp.zeros_like(nxt), nxt)

    rows = CONV_ROWS
    first = halo - CONV_PAD
    for c in range(ts // rows):
        r0 = c * rows
        acc = jnp.zeros((rows, CONV_CH), F32)
        for t in range(CONV_WIDTH):
            s = r0 + first + t
            acc = acc + buf_ref[s:s + rows, :] * w_ref[t:t + 1, :]
        y = acc + cb_ref[...]
        mu = jnp.mean(y, axis=-1, keepdims=True)
        yc = y - mu
        var = jnp.mean(yc * yc, axis=-1, keepdims=True)
        z = yc * lax.rsqrt(var + LN_EPS) * lg_ref[...] + lb_ref[...]
        o_ref[0, r0:r0 + rows, :] = (z * jax.nn.sigmoid(z)).astype(BF16)


def _conv(u, conv_w, conv_b, ln_g, ln_b):
    b, s, c = u.shape
    ts = CONV_TS
    n_blocks = s // ts
    per = ts // CONV_HALO
    n_halo = s // CONV_HALO
    row = lambda v: v.reshape(1, c)
    return pl.pallas_call(
        functools.partial(_conv_kernel, ts=ts, n_blocks=n_blocks),
        grid=(b, n_blocks),
        in_specs=[
            pl.BlockSpec((1, CONV_HALO, c), lambda bi, i: (bi, jnp.maximum(i * per - 1, 0), 0)),
            pl.BlockSpec((1, ts, c), lambda bi, i: (bi, i, 0)),
            pl.BlockSpec((1, CONV_HALO, c), lambda bi, i: (bi, jnp.minimum((i + 1) * per, n_halo - 1), 0)),
            pl.BlockSpec((CONV_WIDTH, c), lambda bi, i: (0, 0)),
            pl.BlockSpec((1, c), lambda bi, i: (0, 0)),
            pl.BlockSpec((1, c), lambda bi, i: (0, 0)),
            pl.BlockSpec((1, c), lambda bi, i: (0, 0)),
        ],
        out_specs=pl.BlockSpec((1, ts, c), lambda bi, i: (bi, i, 0)),
        out_shape=jax.ShapeDtypeStruct((b, s, c), BF16),
        scratch_shapes=[pltpu.VMEM((ts + 2 * CONV_HALO, c), F32)],
        compiler_params=pltpu.CompilerParams(
            dimension_semantics=("arbitrary", "arbitrary"),
            vmem_limit_bytes=VMEM_LIMIT_BYTES),
        name="conv",
    )(u, u, u, conv_w, row(conv_b), row(ln_g), row(ln_b))


def _attn_kernel(q_ref, k_ref, v_ref, bias_ref, c_ref, lq1_ref, lk1_ref, lq2_ref, lk2_ref, g_ref, o_ref,
                 q2_ref, acc_ref, m_ref, l_ref, alpha_ref, s_ref, p_ref, mc_ref, *, tile, n_k):
    i = pl.program_id(2)
    q = q_ref[0]
    lane = lax.broadcasted_iota(jnp.int32, q.shape, 1)
    zero = jnp.zeros_like(q)
    q2_ref[0:tile, :] = jnp.where(lane < HEAD_DK, q, zero)
    q2_ref[tile:, :] = jnp.where(lane >= HEAD_DK, q, zero)
    acc_ref[...] = jnp.zeros_like(acc_ref)
    m_ref[...] = jnp.full_like(m_ref, -jnp.inf)
    l_ref[...] = jnp.zeros_like(l_ref)
    alpha_ref[...] = jnp.ones_like(alpha_ref)
    p_ref[1] = jnp.zeros(p_ref.shape[1:], BF16)

    reps = tile // LANES
    n_far = n_k - N_NEAR
    w0 = jnp.clip(i - (BIAS_SPAN - 1), 0, n_far)

    def tile_of(t):
        far = jnp.where(t < w0, t, t + N_NEAR)
        return jnp.where(t < n_far, far, w0 + (t - n_far))

    def scores(t, slot, near):
        j = tile_of(t)
        k = k_ref[0, pl.ds(pl.multiple_of(j * tile, tile), tile), :]
        s = lax.dot_general(q2_ref[...], k, (((1,), (1,)), ((), ())), preferred_element_type=F32)
        if near:
            bias = bias_ref[0, j - i + BIAS_SPAN]
            s = s + jnp.concatenate([bias, bias], axis=0)
        s_ref[slot] = s
        mc_ref[slot] = jnp.broadcast_to(jnp.max(s, axis=1, keepdims=True), mc_ref.shape[1:])

    def softmax(t, slot, near):
        m_prev = m_ref[...]
        if near:
            m_new = jnp.maximum(m_prev, mc_ref[slot])
            shift = m_new
        else:
            const = jnp.where(tile_of(t) < i, c_ref[0, 0], c_ref[0, 1])
            m_new = jnp.maximum(m_prev, mc_ref[slot] + const)
            shift = m_new - const
        p = jnp.exp2(s_ref[slot] - pltpu.repeat(shift, reps, 1))
        alpha = jnp.exp2(m_prev - m_new)
        l_ref[...] = alpha * l_ref[...] + jnp.sum(p, axis=1, keepdims=True)
        m_ref[...] = m_new
        alpha_ref[...] = alpha
        p_ref[slot] = p.astype(BF16)

    def values(t, slot):
        j = tile_of(jnp.maximum(t, 0))
        v = v_ref[0, pl.ds(pl.multiple_of(j * tile, tile), tile), :]
        acc_ref[...] = alpha_ref[...] * acc_ref[...] + jnp.dot(p_ref[slot], v, preferred_element_type=F32)

    def far_pair(u, carry):
        for parity in range(2):
            t = 2 * u + parity
            values(t - 1, 1 - parity)
            scores(t + 1, 1 - parity, near=False)
            softmax(t, parity, near=False)
        return carry

    n_pairs = max(n_far - 1, 0) // 2
    scores(0, 0, near=(n_far == 0))
    lax.fori_loop(0, n_pairs, far_pair, 0)
    for t in range(2 * n_pairs, n_k):
        parity = t % 2
        values(t - 1, 1 - parity)
        if t + 1 < n_k:
            scores(t + 1, 1 - parity, near=(t + 1 >= n_far))
        softmax(t, parity, near=(t >= n_far))
    values(n_k - 1, (n_k - 1) % 2)

    lam = (jnp.exp(jnp.sum(lq1_ref[...] * lk1_ref[...], axis=-1, keepdims=True))
           - jnp.exp(jnp.sum(lq2_ref[...] * lk2_ref[...], axis=-1, keepdims=True))
           + LAMBDA_INIT)
    o = acc_ref[0:tile, :] / l_ref[0:tile, :] - lam * (acc_ref[tile:, :] / l_ref[tile:, :])
    o = _rms(o, g_ref[...], SUBLN_EPS) * (1.0 - LAMBDA_INIT)
    o_ref[0] = o.astype(BF16)


def _rel_bucket(rel):
    half = N_BUCKETS // 2
    max_exact = half // 2
    ret = (rel > 0).astype(jnp.int32) * half
    n = jnp.abs(rel)
    nf = jnp.maximum(n, 1).astype(jnp.float32)
    large = max_exact + (jnp.log(nf / max_exact) / math.log(MAX_DISTANCE / max_exact)
                         * (half - max_exact)).astype(jnp.int32)
    large = jnp.minimum(large, half - 1)
    return ret + jnp.where(n < max_exact, n, large)


def _bias_tiles(rel_bias, tile):
    assert tile >= MAX_DISTANCE
    lo = -(BIAS_SPAN + 1) * tile
    rel = jnp.arange(lo, -lo, dtype=jnp.int32)
    by_rel = rel_bias.astype(F32)[_rel_bucket(rel)].T * LOG2E
    consts = jnp.stack([by_rel[:, :1], by_rel[:, -1:]], axis=1)
    consts = jnp.broadcast_to(consts[..., None], (N_HEADS, 2, 1, LANES))
    tiles = []
    for d in range(-BIAS_SPAN, BIAS_SPAN + 1):
        pos = lax.slice_in_dim(by_rel, d * tile - lo, (d + 1) * tile - lo, axis=1)
        neg = lax.slice_in_dim(by_rel, (d - 1) * tile - lo, d * tile - lo, axis=1)
        z = jnp.concatenate([pos, neg], axis=1)
        skew = jnp.tile(z, (1, tile))[:, :tile * (2 * tile - 1)].reshape(-1, tile, 2 * tile - 1)
        tiles.append(skew[:, :, :tile])
    return jnp.stack(tiles, axis=1), consts


def _attn(qkv, bias_tiles, bias_consts, lq1, lk1, lq2, lk2, subln_g):
    b, s, _ = qkv.shape
    tile = ATTN_T
    n_k = s // tile
    assert s % tile == 0 and n_k >= N_NEAR
    n_bias = 2 * BIAS_SPAN + 1
    k_col = ATTN_QK // LANES
    v_col = 2 * ATTN_QK // LANES
    vec = lambda v: v.reshape(1, -1).astype(F32)
    small = lambda n: pl.BlockSpec((1, n), lambda h, bi, i: (0, 0))
    rows = 2 * tile
    return pl.pallas_call(
        functools.partial(_attn_kernel, tile=tile, n_k=n_k),
        grid=(N_HEADS, b, s // tile),
        in_specs=[
            pl.BlockSpec((1, tile, LANES), lambda h, bi, i: (bi, i, h)),
            pl.BlockSpec((1, s, LANES), lambda h, bi, i: (bi, 0, k_col + h)),
            pl.BlockSpec((1, s, LANES), lambda h, bi, i: (bi, 0, v_col + h)),
            pl.BlockSpec((1, n_bias, tile, tile), lambda h, bi, i: (h, 0, 0, 0)),
            pl.BlockSpec((1, 2, 1, LANES), lambda h, bi, i: (h, 0, 0, 0)),
            small(HEAD_DK), small(HEAD_DK), small(HEAD_DK), small(HEAD_DK), small(HEAD_DV),
        ],
        out_specs=pl.BlockSpec((1, tile, LANES), lambda h, bi, i: (bi, i, h)),
        out_shape=jax.ShapeDtypeStruct((b, s, ATTN_V), BF16),
        scratch_shapes=[
            pltpu.VMEM((rows, LANES), BF16),
            pltpu.VMEM((rows, HEAD_DV), F32),
            pltpu.VMEM((rows, LANES), F32),
            pltpu.VMEM((rows, LANES), F32),
            pltpu.VMEM((rows, LANES), F32),
            pltpu.VMEM((2, rows, tile), F32),
            pltpu.VMEM((2, rows, tile), BF16),
            pltpu.VMEM((2, rows, LANES), F32),
        ],
        compiler_params=pltpu.CompilerParams(
            dimension_semantics=("arbitrary", "arbitrary", "arbitrary"),
            vmem_limit_bytes=VMEM_LIMIT_BYTES),
        name="diff_attn",
    )(qkv, qkv, qkv, bias_tiles, bias_consts, vec(lq1), vec(lk1), vec(lq2), vec(lk2), vec(subln_g))


def _outproj_kernel(x_ref, c_ref, a_ref, wc_ref, wa_ref, o_ref):
    o_ref[...] = (x_ref[...]
                  + jnp.dot(c_ref[...], wc_ref[...], preferred_element_type=F32)
                  + jnp.dot(a_ref[...], wa_ref[...], preferred_element_type=F32))


def _outproj(x, conv_out, att, w_out):
    m, d = x.shape
    tm = PROJ_TM
    return pl.pallas_call(
        _outproj_kernel,
        grid=(m // tm,),
        in_specs=[
            pl.BlockSpec((tm, d), lambda i: (i, 0)),
            pl.BlockSpec((tm, CONV_CH), lambda i: (i, 0)),
            pl.BlockSpec((tm, ATTN_V), lambda i: (i, 0)),
            pl.BlockSpec((CONV_CH, d), lambda i: (0, 0)),
            pl.BlockSpec((ATTN_V, d), lambda i: (1, 0)),
        ],
        out_specs=pl.BlockSpec((tm, d), lambda i: (i, 0)),
        out_shape=jax.ShapeDtypeStruct((m, d), F32),
        compiler_params=pltpu.CompilerParams(
            dimension_semantics=("arbitrary",),
            vmem_limit_bytes=VMEM_LIMIT_BYTES),
        name="outproj",
    )(x, conv_out, att, w_out, w_out)


def _trunk(x, w, bias):
    b, s, d = x.shape
    m = b * s
    x0 = x.reshape(m, d)
    x1 = _ffn(x0, w["ffn1_norm"], w["ffn1_w_in"], w["ffn1_w_out"])
    u, qkv = _inproj(x1, w["mix_norm"], w["w_in"])
    conv_out = _conv(u.reshape(b, s, CONV_CH), w["conv_w"], w["conv_b"], w["conv_ln_g"], w["conv_ln_b"])
    att = _attn(qkv.reshape(b, s, -1), *bias, w["lambda_q1"], w["lambda_k1"],
                w["lambda_q2"], w["lambda_k2"], w["subln_g"])
    x2 = _outproj(x1, conv_out.reshape(m, CONV_CH), att.reshape(m, ATTN_V), w["w_out"])
    y = _ffn(x2, w["ffn2_norm"], w["ffn2_w_in"], w["ffn2_w_out"], final_g=w["final_norm"])
    return y.reshape(b, s, d)


def kernel(x_prompt, x_sample, rel_bias, ffn1_norm, ffn1_w_in, ffn1_w_out, mix_norm, w_in, conv_w, conv_b, conv_ln_g, conv_ln_b, lambda_q1, lambda_k1, lambda_q2, lambda_k2, subln_g, w_out, ffn2_norm, ffn2_w_in, ffn2_w_out, final_norm):
    assert ffn1_norm.shape[0] == 1, "single-layer trunk"
    w = dict(
        ffn1_norm=ffn1_norm[0], ffn1_w_in=ffn1_w_in[0].astype(BF16), ffn1_w_out=ffn1_w_out[0].astype(BF16),
        mix_norm=mix_norm[0], w_in=w_in[0].astype(BF16),
        conv_w=conv_w[0], conv_b=conv_b[0], conv_ln_g=conv_ln_g[0], conv_ln_b=conv_ln_b[0],
        lambda_q1=lambda_q1[0], lambda_k1=lambda_k1[0], lambda_q2=lambda_q2[0], lambda_k2=lambda_k2[0],
        subln_g=subln_g[0], w_out=w_out[0].astype(BF16),
        ffn2_norm=ffn2_norm[0], ffn2_w_in=ffn2_w_in[0].astype(BF16), ffn2_w_out=ffn2_w_out[0].astype(BF16),
        final_norm=final_norm,
    )
    bias = _bias_tiles(rel_bias, ATTN_T)
    return (_trunk(x_prompt, w, bias), _trunk(x_sample, w, bias))
```

```python
import functools
import math

import jax
import jax.numpy as jnp
from jax import lax
from jax.experimental import pallas as pl
from jax.experimental.pallas import tpu as pltpu

F32 = jnp.float32
BF16 = jnp.bfloat16

D_MODEL = 2048
CONV_CH = D_MODEL // 2
CONV_WIDTH = 31
CONV_PAD = CONV_WIDTH // 2
N_HEADS = 8
HEAD_DK = 64
HEAD_DV = 2 * HEAD_DK
ATTN_QK = N_HEADS * 2 * HEAD_DK
ATTN_V = N_HEADS * HEAD_DV
D_FF = 5632
N_BUCKETS = 32
MAX_DISTANCE = 128
RMS_EPS = 1e-6
LN_EPS = 1e-5
SUBLN_EPS = 1e-5
LAYER = 0
LAMBDA_INIT = 0.8 - 0.6 * math.exp(-0.3 * LAYER)
LOG2E = math.log2(math.e)
Q_SCALE = HEAD_DK ** -0.5 * LOG2E

LANES = 128
SUBLANES = 8
VMEM_LIMIT_BYTES = 56 * 1024 * 1024

FFN_TM = 512
FFN_TF = 512
PROJ_TM = 512
CONV_TS = 512
CONV_HALO = 16
CONV_ROWS = 64
CONV_LANES = 256
ATTN_T = 512
BIAS_SPAN = 2
N_NEAR = 2 * BIAS_SPAN - 1


def _rms(x, g, eps):
    return x * lax.rsqrt(jnp.mean(x * x, axis=-1, keepdims=True) + eps) * g


def _ffn_kernel(x_ref, g_ref, wg_ref, wu_ref, wo_ref, *rest, n_steps, final):
    if final:
        fg_ref, o_ref, xn_ref, acc_ref = rest
    else:
        o_ref, xn_ref, acc_ref = rest
    j = pl.program_id(1)

    @pl.when(j == 0)
    def _():
        xn_ref[...] = _rms(x_ref[...], g_ref[...], RMS_EPS).astype(BF16)
        acc_ref[...] = jnp.zeros_like(acc_ref)

    xn = xn_ref[...]
    gate = jnp.dot(xn, wg_ref[...], preferred_element_type=F32)
    up = jnp.dot(xn, wu_ref[...], preferred_element_type=F32)
    act = (gate * jax.nn.sigmoid(gate) * up).astype(BF16)
    acc_ref[...] += jnp.dot(act, wo_ref[...], preferred_element_type=F32)

    @pl.when(j == n_steps - 1)
    def _():
        y = x_ref[...] + 0.5 * acc_ref[...]
        if final:
            y = _rms(y, fg_ref[...], RMS_EPS)
        o_ref[...] = y


def _ffn(x, norm_g, w_in, w_out, final_g=None):
    m, d = x.shape
    tm, tf = FFN_TM, FFN_TF
    n_steps = D_FF // tf
    final = final_g is not None
    in_specs = [
        pl.BlockSpec((tm, d), lambda i, j: (i, 0)),
        pl.BlockSpec((1, d), lambda i, j: (0, 0)),
        pl.BlockSpec((d, tf), lambda i, j: (0, j)),
        pl.BlockSpec((d, tf), lambda i, j: (0, j + n_steps)),
        pl.BlockSpec((tf, d), lambda i, j: (j, 0)),
    ]
    args = [x, norm_g.reshape(1, d), w_in, w_in, w_out]
    if final:
        in_specs.append(pl.BlockSpec((1, d), lambda i, j: (0, 0)))
        args.append(final_g.reshape(1, d))
    return pl.pallas_call(
        functools.partial(_ffn_kernel, n_steps=n_steps, final=final),
        grid=(m // tm, n_steps),
        in_specs=in_specs,
        out_specs=pl.BlockSpec((tm, d), lambda i, j: (i, 0)),
        out_shape=jax.ShapeDtypeStruct((m, d), F32),
        scratch_shapes=[pltpu.VMEM((tm, d), BF16), pltpu.VMEM((tm, d), F32)],
        compiler_params=pltpu.CompilerParams(
            dimension_semantics=("arbitrary", "arbitrary"),
            vmem_limit_bytes=VMEM_LIMIT_BYTES),
        name="ffn_final" if final else "ffn",
    )(*args)


def _inproj_kernel(x_ref, g_ref, wa_ref, wgate_ref, wqkv_ref, u_ref, qkv_ref):
    xn = _rms(x_ref[...], g_ref[...], RMS_EPS).astype(BF16)
    a = jnp.dot(xn, wa_ref[...], preferred_element_type=F32)
    gate = jnp.dot(xn, wgate_ref[...], preferred_element_type=F32)
    u_ref[...] = a * jax.nn.sigmoid(gate)
    for c, scale in enumerate((Q_SCALE, None, None)):
        cols = slice(c * ATTN_QK, (c + 1) * ATTN_QK)
        h = jnp.dot(xn, wqkv_ref[:, cols], preferred_element_type=F32)
        qkv_ref[:, cols] = (h if scale is None else h * scale).astype(BF16)


def _inproj(x, norm_g, w_in):
    m, d = x.shape
    tm = PROJ_TM
    n_qkv = 2 * ATTN_QK + ATTN_V
    return pl.pallas_call(
        _inproj_kernel,
        grid=(m // tm,),
        in_specs=[
            pl.BlockSpec((tm, d), lambda i: (i, 0)),
            pl.BlockSpec((1, d), lambda i: (0, 0)),
            pl.BlockSpec((d, CONV_CH), lambda i: (0, 0)),
            pl.BlockSpec((d, CONV_CH), lambda i: (0, 1)),
            pl.BlockSpec((d, n_qkv), lambda i: (0, 0)),
        ],
        out_specs=[
            pl.BlockSpec((tm, CONV_CH), lambda i: (i, 0)),
            pl.BlockSpec((tm, n_qkv), lambda i: (i, 0)),
        ],
        out_shape=[
            jax.ShapeDtypeStruct((m, CONV_CH), F32),
            jax.ShapeDtypeStruct((m, n_qkv), BF16),
        ],
        compiler_params=pltpu.CompilerParams(
            dimension_semantics=("arbitrary",),
            vmem_limit_bytes=VMEM_LIMIT_BYTES),
        name="inproj",
    )(x, norm_g.reshape(1, d), w_in, w_in, w_in[:, 2 * CONV_CH:])


def _conv_kernel(prev_ref, cur_ref, next_ref, w_ref, cb_ref, lg_ref, lb_ref, o_ref, buf_ref,
                 *, ts, n_blocks):
    i = pl.program_id(1)
    halo = CONV_HALO
    prev = prev_ref[0]
    nxt = next_ref[0]
    buf_ref[0:halo, :] = jnp.where(i == 0, jnp.zeros_like(prev), prev)
    buf_ref[halo:halo + ts, :] = cur_ref[0]
    buf_ref[halo + ts:, :] = jnp.where(i == n_blocks - 1, jnp.zeros_like(nxt), nxt)

    rows = CONV_ROWS
    first = halo - CONV_PAD
    for c in range(ts // rows):
        r0 = c * rows
        parts = []
        for g in range(CONV_CH // CONV_LANES):
            lanes = slice(g * CONV_LANES, (g + 1) * CONV_LANES)
            y = None
            for b in range(SUBLANES):
                zb = None
                for a in range((CONV_WIDTH - b + SUBLANES - 1) // SUBLANES):
                    t = SUBLANES * a + b
                    lo = r0 + SUBLANES * a
                    w_tap = jnp.concatenate([w_ref[t, :, lanes]] * (rows // SUBLANES + 1), axis=0)
                    term = buf_ref[lo:lo + rows + SUBLANES, lanes] * w_tap
                    zb = term if zb is None else zb + term
                k = first + b
                if k % SUBLANES:
                    zb = pltpu.roll(zb, rows + SUBLANES - k, 0)
                    k = 0
                shifted = zb[k:k + rows, :]
                y = shifted if y is None else y + shifted
            parts.append(y)
        y = jnp.concatenate(parts, axis=1) + cb_ref[...]
        mu = jnp.mean(y, axis=-1, keepdims=True)
        yc = y - mu
        var = jnp.mean(yc * yc, axis=-1, keepdims=True)
        z = yc * lax.rsqrt(var + LN_EPS) * lg_ref[...] + lb_ref[...]
        o_ref[0, r0:r0 + rows, :] = (z * jax.nn.sigmoid(z)).astype(BF16)


def _conv(u, conv_w, conv_b, ln_g, ln_b):
    b, s, c = u.shape
    ts = CONV_TS
    n_blocks = s // ts
    per = ts // CONV_HALO
    n_halo = s // CONV_HALO
    row = lambda v: v.reshape(1, c)
    return pl.pallas_call(
        functools.partial(_conv_kernel, ts=ts, n_blocks=n_blocks),
        grid=(b, n_blocks),
        in_specs=[
            pl.BlockSpec((1, CONV_HALO, c), lambda bi, i: (bi, jnp.maximum(i * per - 1, 0), 0)),
            pl.BlockSpec((1, ts, c), lambda bi, i: (bi, i, 0)),
            pl.BlockSpec((1, CONV_HALO, c), lambda bi, i: (bi, jnp.minimum((i + 1) * per, n_halo - 1), 0)),
            pl.BlockSpec((CONV_WIDTH, SUBLANES, c), lambda bi, i: (0, 0, 0)),
            pl.BlockSpec((1, c), lambda bi, i: (0, 0)),
            pl.BlockSpec((1, c), lambda bi, i: (0, 0)),
            pl.BlockSpec((1, c), lambda bi, i: (0, 0)),
        ],
        out_specs=pl.BlockSpec((1, ts, c), lambda bi, i: (bi, i, 0)),
        out_shape=jax.ShapeDtypeStruct((b, s, c), BF16),
        scratch_shapes=[pltpu.VMEM((ts + 2 * CONV_HALO, c), F32)],
        compiler_params=pltpu.CompilerParams(
            dimension_semantics=("arbitrary", "arbitrary"),
            vmem_limit_bytes=VMEM_LIMIT_BYTES),
        name="conv",
    )(u, u, u, jnp.broadcast_to(conv_w[:, None, :], (CONV_WIDTH, SUBLANES, c)),
      row(conv_b), row(ln_g), row(ln_b))


def _attn_kernel(q_ref, k_ref, v_ref, bias_ref, c_ref, lq1_ref, lk1_ref, lq2_ref, lk2_ref, g_ref, o_ref,
                 q2_ref, acc_ref, m_ref, alpha_ref, s_ref, p_ref, mc_ref, *, tile, n_k):
    i = pl.program_id(2)
    q = q_ref[0]
    lane = lax.broadcasted_iota(jnp.int32, q.shape, 1)
    zero = jnp.zeros_like(q)
    q2_ref[0:tile, :] = jnp.where(lane < HEAD_DK, q, zero)
    q2_ref[tile:, :] = jnp.where(lane >= HEAD_DK, q, zero)
    acc_ref[...] = jnp.zeros_like(acc_ref)
    m_ref[...] = jnp.full_like(m_ref, -jnp.inf)
    alpha_ref[...] = jnp.ones_like(alpha_ref)
    p_ref[1] = jnp.zeros(p_ref.shape[1:], BF16)

    reps = tile // LANES
    n_far = n_k - N_NEAR
    w0 = jnp.clip(i - (BIAS_SPAN - 1), 0, n_far)

    def tile_of(t):
        far = jnp.where(t < w0, t, t + N_NEAR)
        return jnp.where(t < n_far, far, w0 + (t - n_far))

    def scores(t, slot, near):
        j = tile_of(t)
        k = k_ref[0, pl.ds(pl.multiple_of(j * tile, tile), tile), :]
        s = lax.dot_general(q2_ref[...], k, (((1,), (1,)), ((), ())), preferred_element_type=F32)
        if near:
            bias = bias_ref[0, j - i + BIAS_SPAN]
            s = s + jnp.concatenate([bias, bias], axis=0)
        s_ref[slot] = s
        mc_ref[slot] = jnp.broadcast_to(jnp.max(s, axis=1, keepdims=True), mc_ref.shape[1:])

    def softmax(t, slot, near):
        m_prev = m_ref[...]
        if near:
            m_new = jnp.maximum(m_prev, mc_ref[slot])
            shift = m_new
        else:
            const = jnp.where(tile_of(t) < i, c_ref[0, 0], c_ref[0, 1])
            m_new = jnp.maximum(m_prev, mc_ref[slot] + const)
            shift = m_new - const
        p = jnp.exp2(s_ref[slot] - jnp.concatenate([shift] * reps, axis=1))
        alpha = jnp.exp2(m_prev - m_new)
        m_ref[...] = m_new
        alpha_ref[...] = alpha
        p_ref[slot] = p.astype(BF16)

    def values(t, slot):
        j = tile_of(jnp.maximum(t, 0))
        v = v_ref[0, pl.ds(pl.multiple_of(j * tile, tile), tile), :]
        v_ones = jnp.concatenate([v, jnp.ones_like(v)], axis=1)
        alpha = jnp.concatenate([alpha_ref[...]] * 2, axis=1)
        acc_ref[...] = alpha * acc_ref[...] + jnp.dot(p_ref[slot], v_ones, preferred_element_type=F32)

    def far_pair(u, carry):
        for parity in range(2):
            t = 2 * u + parity
            values(t - 1, 1 - parity)
            scores(t + 1, 1 - parity, near=False)
            softmax(t, parity, near=False)
        return carry

    n_pairs = max(n_far - 1, 0) // 2
    scores(0, 0, near=(n_far == 0))
    lax.fori_loop(0, n_pairs, far_pair, 0)
    for t in range(2 * n_pairs, n_k):
        parity = t % 2
        values(t - 1, 1 - parity)
        if t + 1 < n_k:
            scores(t + 1, 1 - parity, near=(t + 1 >= n_far))
        softmax(t, parity, near=(t >= n_far))
    values(n_k - 1, (n_k - 1) % 2)

    lam = (jnp.exp(jnp.sum(lq1_ref[...] * lk1_ref[...], axis=-1, keepdims=True))
           - jnp.exp(jnp.sum(lq2_ref[...] * lk2_ref[...], axis=-1, keepdims=True))
           + LAMBDA_INIT)
    o1 = acc_ref[0:tile, 0:HEAD_DV] / acc_ref[0:tile, HEAD_DV:]
    o2 = acc_ref[tile:, 0:HEAD_DV] / acc_ref[tile:, HEAD_DV:]
    o = o1 - lam * o2
    o = _rms(o, g_ref[...], SUBLN_EPS) * (1.0 - LAMBDA_INIT)
    o_ref[0] = o.astype(BF16)


def _rel_bucket(rel):
    half = N_BUCKETS // 2
    max_exact = half // 2
    ret = (rel > 0).astype(jnp.int32) * half
    n = jnp.abs(rel)
    nf = jnp.maximum(n, 1).astype(jnp.float32)
    large = max_exact + (jnp.log(nf / max_exact) / math.log(MAX_DISTANCE / max_exact)
                         * (half - max_exact)).astype(jnp.int32)
    large = jnp.minimum(large, half - 1)
    return ret + jnp.where(n < max_exact, n, large)


def _bias_tiles(rel_bias, tile):
    assert tile >= MAX_DISTANCE
    lo = -(BIAS_SPAN + 1) * tile
    rel = jnp.arange(lo, -lo, dtype=jnp.int32)
    by_rel = rel_bias.astype(F32)[_rel_bucket(rel)].T * LOG2E
    consts = jnp.stack([by_rel[:, :1], by_rel[:, -1:]], axis=1)
    consts = jnp.broadcast_to(consts[..., None], (N_HEADS, 2, 1, LANES))
    tiles = []
    for d in range(-BIAS_SPAN, BIAS_SPAN + 1):
        pos = lax.slice_in_dim(by_rel, d * tile - lo, (d + 1) * tile - lo, axis=1)
        neg = lax.slice_in_dim(by_rel, (d - 1) * tile - lo, d * tile - lo, axis=1)
        z = jnp.concatenate([pos, neg], axis=1)
        skew = jnp.tile(z, (1, tile))[:, :tile * (2 * tile - 1)].reshape(-1, tile, 2 * tile - 1)
        tiles.append(skew[:, :, :tile])
    return jnp.stack(tiles, axis=1), consts


def _attn(qkv, bias_tiles, bias_consts, lq1, lk1, lq2, lk2, subln_g):
    b, s, _ = qkv.shape
    tile = ATTN_T
    n_k = s // tile
    assert s % tile == 0 and n_k >= N_NEAR
    n_bias = 2 * BIAS_SPAN + 1
    k_col = ATTN_QK // LANES
    v_col = 2 * ATTN_QK // LANES
    vec = lambda v: v.reshape(1, -1).astype(F32)
    small = lambda n: pl.BlockSpec((1, n), lambda h, bi, i: (0, 0))
    rows = 2 * tile
    return pl.pallas_call(
        functools.partial(_attn_kernel, tile=tile, n_k=n_k),
        grid=(N_HEADS, b, s // tile),
        in_specs=[
            pl.BlockSpec((1, tile, LANES), lambda h, bi, i: (bi, i, h)),
            pl.BlockSpec((1, s, LANES), lambda h, bi, i: (bi, 0, k_col + h)),
            pl.BlockSpec((1, s, LANES), lambda h, bi, i: (bi, 0, v_col + h)),
            pl.BlockSpec((1, n_bias, tile, tile), lambda h, bi, i: (h, 0, 0, 0)),
            pl.BlockSpec((1, 2, 1, LANES), lambda h, bi, i: (h, 0, 0, 0)),
            small(HEAD_DK), small(HEAD_DK), small(HEAD_DK), small(HEAD_DK), small(HEAD_DV),
        ],
        out_specs=pl.BlockSpec((1, tile, LANES), lambda h, bi, i: (bi, i, h)),
        out_shape=jax.ShapeDtypeStruct((b, s, ATTN_V), BF16),
        scratch_shapes=[
            pltpu.VMEM((rows, LANES), BF16),
            pltpu.VMEM((rows, 2 * HEAD_DV), F32),
            pltpu.VMEM((rows, LANES), F32),
            pltpu.VMEM((rows, LANES), F32),
            pltpu.VMEM((2, rows, tile), F32),
            pltpu.VMEM((2, rows, tile), BF16),
            pltpu.VMEM((2, rows, LANES), F32),
        ],
        compiler_params=pltpu.CompilerParams(
            dimension_semantics=("arbitrary", "arbitrary", "arbitrary"),
            vmem_limit_bytes=VMEM_LIMIT_BYTES),
        name="diff_attn",
    )(qkv, qkv, qkv, bias_tiles, bias_consts, vec(lq1), vec(lk1), vec(lq2), vec(lk2), vec(subln_g))


def _outproj_kernel(x_ref, c_ref, a_ref, wc_ref, wa_ref, o_ref):
    o_ref[...] = (x_ref[...]
                  + jnp.dot(c_ref[...], wc_ref[...], preferred_element_type=F32)
                  + jnp.dot(a_ref[...], wa_ref[...], preferred_element_type=F32))


def _outproj(x, conv_out, att, w_out):
    m, d = x.shape
    tm = PROJ_TM
    return pl.pallas_call(
        _outproj_kernel,
        grid=(m // tm,),
        in_specs=[
            pl.BlockSpec((tm, d), lambda i: (i, 0)),
            pl.BlockSpec((tm, CONV_CH), lambda i: (i, 0)),
            pl.BlockSpec((tm, ATTN_V), lambda i: (i, 0)),
            pl.BlockSpec((CONV_CH, d), lambda i: (0, 0)),
            pl.BlockSpec((ATTN_V, d), lambda i: (1, 0)),
        ],
        out_specs=pl.BlockSpec((tm, d), lambda i: (i, 0)),
        out_shape=jax.ShapeDtypeStruct((m, d), F32),
        compiler_params=pltpu.CompilerParams(
            dimension_semantics=("arbitrary",),
            vmem_limit_bytes=VMEM_LIMIT_BYTES),
        name="outproj",
    )(x, conv_out, att, w_out, w_out)


def _trunk(x, w, bias):
    b, s, d = x.shape
    m = b * s
    x0 = x.reshape(m, d)
    x1 = _ffn(x0, w["ffn1_norm"], w["ffn1_w_in"], w["ffn1_w_out"])
    u, qkv = _inproj(x1, w["mix_norm"], w["w_in"])
    conv_out = _conv(u.reshape(b, s, CONV_CH), w["conv_w"], w["conv_b"], w["conv_ln_g"], w["conv_ln_b"])
    att = _attn(qkv.reshape(b, s, -1), *bias, w["lambda_q1"], w["lambda_k1"],
                w["lambda_q2"], w["lambda_k2"], w["subln_g"])
    x2 = _outproj(x1, conv_out.reshape(m, CONV_CH), att.reshape(m, ATTN_V), w["w_out"])
    y = _ffn(x2, w["ffn2_norm"], w["ffn2_w_in"], w["ffn2_w_out"], final_g=w["final_norm"])
    return y.reshape(b, s, d)


def kernel(x_prompt, x_sample, rel_bias, ffn1_norm, ffn1_w_in, ffn1_w_out, mix_norm, w_in, conv_w, conv_b, conv_ln_g, conv_ln_b, lambda_q1, lambda_k1, lambda_q2, lambda_k2, subln_g, w_out, ffn2_norm, ffn2_w_in, ffn2_w_out, final_norm):
    assert ffn1_norm.shape[0] == 1, "single-layer trunk"
    w = dict(
        ffn1_norm=ffn1_norm[0], ffn1_w_in=ffn1_w_in[0].astype(BF16), ffn1_w_out=ffn1_w_out[0].astype(BF16),
        mix_norm=mix_norm[0], w_in=w_in[0].astype(BF16),
        conv_w=conv_w[0], conv_b=conv_b[0], conv_ln_g=conv_ln_g[0], conv_ln_b=conv_ln_b[0],
        lambda_q1=lambda_q1[0], lambda_k1=lambda_k1[0], lambda_q2=lambda_q2[0], lambda_k2=lambda_k2[0],
        subln_g=subln_g[0], w_out=w_out[0].astype(BF16),
        ffn2_norm=ffn2_norm[0], ffn2_w_in=ffn2_w_in[0].astype(BF16), ffn2_w_out=ffn2_w_out[0].astype(BF16),
        final_norm=final_norm,
    )
    bias = _bias_tiles(rel_bias, ATTN_T)
    return (_trunk(x_prompt, w, bias), _trunk(x_sample, w, bias))
```

```python
import functools
import math

import jax
import jax.numpy as jnp
from jax import lax
from jax.experimental import pallas as pl
from jax.experimental.pallas import tpu as pltpu

F32 = jnp.float32
BF16 = jnp.bfloat16

D_MODEL = 2048
CONV_CH = D_MODEL // 2
CONV_WIDTH = 31
CONV_PAD = CONV_WIDTH // 2
N_HEADS = 8
HEAD_DK = 64
HEAD_DV = 2 * HEAD_DK
ATTN_QK = N_HEADS * 2 * HEAD_DK
ATTN_V = N_HEADS * HEAD_DV
D_FF = 5632
N_BUCKETS = 32
MAX_DISTANCE = 128
RMS_EPS = 1e-6
LN_EPS = 1e-5
SUBLN_EPS = 1e-5
LAYER = 0
LAMBDA_INIT = 0.8 - 0.6 * math.exp(-0.3 * LAYER)
LOG2E = math.log2(math.e)
Q_SCALE = HEAD_DK ** -0.5 * LOG2E

LANES = 128
SUBLANES = 8
VMEM_LIMIT_BYTES = 56 * 1024 * 1024

FFN_TM = 512
FFN_TF = 512
PROJ_TM = 512
CONV_TS = 512
CONV_HALO = 16
CONV_ROWS = 64
CONV_LANES = 256
ATTN_T = 512
BIAS_SPAN = 2
ATTN_WIDE = 2
ATTN_NEAR = 4


def _rms(x, g, eps):
    return x * lax.rsqrt(jnp.mean(x * x, axis=-1, keepdims=True) + eps) * g


def _ffn_kernel(x_ref, g_ref, wg_ref, wu_ref, wo_ref, *rest, n_steps, final):
    if final:
        fg_ref, o_ref, xn_ref, acc_ref = rest
    else:
        o_ref, xn_ref, acc_ref = rest
    j = pl.program_id(1)

    @pl.when(j == 0)
    def _():
        xn_ref[...] = _rms(x_ref[...], g_ref[...], RMS_EPS).astype(BF16)
        acc_ref[...] = jnp.zeros_like(acc_ref)

    xn = xn_ref[...]
    gate = jnp.dot(xn, wg_ref[...], preferred_element_type=F32)
    up = jnp.dot(xn, wu_ref[...], preferred_element_type=F32)
    act = (gate * jax.nn.sigmoid(gate) * up).astype(BF16)
    acc_ref[...] += jnp.dot(act, wo_ref[...], preferred_element_type=F32)

    @pl.when(j == n_steps - 1)
    def _():
        y = x_ref[...] + 0.5 * acc_ref[...]
        if final:
            y = _rms(y, fg_ref[...], RMS_EPS)
        o_ref[...] = y


def _ffn(x, norm_g, w_in, w_out, final_g=None):
    m, d = x.shape
    tm, tf = FFN_TM, FFN_TF
    n_steps = D_FF // tf
    final = final_g is not None
    in_specs = [
        pl.BlockSpec((tm, d), lambda i, j: (i, 0)),
        pl.BlockSpec((1, d), lambda i, j: (0, 0)),
        pl.BlockSpec((d, tf), lambda i, j: (0, j)),
        pl.BlockSpec((d, tf), lambda i, j: (0, j + n_steps)),
        pl.BlockSpec((tf, d), lambda i, j: (j, 0)),
    ]
    args = [x, norm_g.reshape(1, d), w_in, w_in, w_out]
    if final:
        in_specs.append(pl.BlockSpec((1, d), lambda i, j: (0, 0)))
        args.append(final_g.reshape(1, d))
    return pl.pallas_call(
        functools.partial(_ffn_kernel, n_steps=n_steps, final=final),
        grid=(m // tm, n_steps),
        in_specs=in_specs,
        out_specs=pl.BlockSpec((tm, d), lambda i, j: (i, 0)),
        out_shape=jax.ShapeDtypeStruct((m, d), F32),
        scratch_shapes=[pltpu.VMEM((tm, d), BF16), pltpu.VMEM((tm, d), F32)],
        compiler_params=pltpu.CompilerParams(
            dimension_semantics=("arbitrary", "arbitrary"),
            vmem_limit_bytes=VMEM_LIMIT_BYTES),
        name="ffn_final" if final else "ffn",
    )(*args)


def _inproj_kernel(x_ref, g_ref, wa_ref, wgate_ref, wqkv_ref, u_ref, qkv_ref):
    xn = _rms(x_ref[...], g_ref[...], RMS_EPS).astype(BF16)
    a = jnp.dot(xn, wa_ref[...], preferred_element_type=F32)
    gate = jnp.dot(xn, wgate_ref[...], preferred_element_type=F32)
    u_ref[...] = a * jax.nn.sigmoid(gate)
    for c, scale in enumerate((Q_SCALE, None, None)):
        cols = slice(c * ATTN_QK, (c + 1) * ATTN_QK)
        h = jnp.dot(xn, wqkv_ref[:, cols], preferred_element_type=F32)
        qkv_ref[:, cols] = (h if scale is None else h * scale).astype(BF16)


def _inproj(x, norm_g, w_in):
    m, d = x.shape
    tm = PROJ_TM
    n_qkv = 2 * ATTN_QK + ATTN_V
    return pl.pallas_call(
        _inproj_kernel,
        grid=(m // tm,),
        in_specs=[
            pl.BlockSpec((tm, d), lambda i: (i, 0)),
            pl.BlockSpec((1, d), lambda i: (0, 0)),
            pl.BlockSpec((d, CONV_CH), lambda i: (0, 0)),
            pl.BlockSpec((d, CONV_CH), lambda i: (0, 1)),
            pl.BlockSpec((d, n_qkv), lambda i: (0, 0)),
        ],
        out_specs=[
            pl.BlockSpec((tm, CONV_CH), lambda i: (i, 0)),
            pl.BlockSpec((tm, n_qkv), lambda i: (i, 0)),
        ],
        out_shape=[
            jax.ShapeDtypeStruct((m, CONV_CH), F32),
            jax.ShapeDtypeStruct((m, n_qkv), BF16),
        ],
        compiler_params=pltpu.CompilerParams(
            dimension_semantics=("arbitrary",),
            vmem_limit_bytes=VMEM_LIMIT_BYTES),
        name="inproj",
    )(x, norm_g.reshape(1, d), w_in, w_in, w_in[:, 2 * CONV_CH:])


def _conv_kernel(prev_ref, cur_ref, next_ref, w_ref, cb_ref, lg_ref, lb_ref, o_ref, buf_ref,
                 *, ts, n_blocks):
    i = pl.program_id(1)
    halo = CONV_HALO
    prev = prev_ref[0]
    nxt = next_ref[0]
    buf_ref[0:halo, :] = jnp.where(i == 0, jnp.zeros_like(prev), prev)
    buf_ref[halo:halo + ts, :] = cur_ref[0]
    buf_ref[halo + ts:, :] = jnp.where(i == n_blocks - 1, jnp.zeros_like(nxt), nxt)

    rows = CONV_ROWS
    first = halo - CONV_PAD
    for c in range(ts // rows):
        r0 = c * rows
        parts = []
        for g in range(CONV_CH // CONV_LANES):
            lanes = slice(g * CONV_LANES, (g + 1) * CONV_LANES)
            y = None
            for b in range(SUBLANES):
                zb = None
                for a in range((CONV_WIDTH - b + SUBLANES - 1) // SUBLANES):
                    t = SUBLANES * a + b
                    lo = r0 + SUBLANES * a
                    w_tap = jnp.concatenate([w_ref[t, :, lanes]] * (rows // SUBLANES + 1), axis=0)
                    term = buf_ref[lo:lo + rows + SUBLANES, lanes] * w_tap
                    zb = term if zb is None else zb + term
                k = first + b
                if k % SUBLANES:
                    zb = pltpu.roll(zb, rows + SUBLANES - k, 0)
                    k = 0
                shifted = zb[k:k + rows, :]
                y = shifted if y is None else y + shifted
            parts.append(y)
        y = jnp.concatenate(parts, axis=1) + cb_ref[...]
        mu = jnp.mean(y, axis=-1, keepdims=True)
        yc = y - mu
        var = jnp.mean(yc * yc, axis=-1, keepdims=True)
        z = yc * lax.rsqrt(var + LN_EPS) * lg_ref[...] + lb_ref[...]
        o_ref[0, r0:r0 + rows, :] = (z * jax.nn.sigmoid(z)).astype(BF16)


def _conv(u, conv_w, conv_b, ln_g, ln_b):
    b, s, c = u.shape
    ts = CONV_TS
    n_blocks = s // ts
    per = ts // CONV_HALO
    n_halo = s // CONV_HALO
    row = lambda v: v.reshape(1, c)
    return pl.pallas_call(
        functools.partial(_conv_kernel, ts=ts, n_blocks=n_blocks),
        grid=(b, n_blocks),
        in_specs=[
            pl.BlockSpec((1, CONV_HALO, c), lambda bi, i: (bi, jnp.maximum(i * per - 1, 0), 0)),
            pl.BlockSpec((1, ts, c), lambda bi, i: (bi, i, 0)),
            pl.BlockSpec((1, CONV_HALO, c), lambda bi, i: (bi, jnp.minimum((i + 1) * per, n_halo - 1), 0)),
            pl.BlockSpec((CONV_WIDTH, SUBLANES, c), lambda bi, i: (0, 0, 0)),
            pl.BlockSpec((1, c), lambda bi, i: (0, 0)),
            pl.BlockSpec((1, c), lambda bi, i: (0, 0)),
            pl.BlockSpec((1, c), lambda bi, i: (0, 0)),
        ],
        out_specs=pl.BlockSpec((1, ts, c), lambda bi, i: (bi, i, 0)),
        out_shape=jax.ShapeDtypeStruct((b, s, c), BF16),
        scratch_shapes=[pltpu.VMEM((ts + 2 * CONV_HALO, c), F32)],
        compiler_params=pltpu.CompilerParams(
            dimension_semantics=("arbitrary", "arbitrary"),
            vmem_limit_bytes=VMEM_LIMIT_BYTES),
        name="conv",
    )(u, u, u, jnp.broadcast_to(conv_w[:, None, :], (CONV_WIDTH, SUBLANES, c)),
      row(conv_b), row(ln_g), row(ln_b))


def _attn_kernel(q_ref, k_ref, v_ref, bias_ref, c_ref, lq1_ref, lk1_ref, lq2_ref, lk2_ref, g_ref, o_ref,
                 q2_ref, acc_ref, m_ref, alpha_ref, s_ref, p_ref, mc_ref, *, tile, n_k):
    i = pl.program_id(2)
    q = q_ref[0]
    lane = lax.broadcasted_iota(jnp.int32, q.shape, 1)
    zero = jnp.zeros_like(q)
    q2_ref[0:tile, :] = jnp.where(lane < HEAD_DK, q, zero)
    q2_ref[tile:, :] = jnp.where(lane >= HEAD_DK, q, zero)
    acc_ref[...] = jnp.zeros_like(acc_ref)
    m_ref[...] = jnp.full_like(m_ref, -jnp.inf)

    wide = ATTN_WIDE * tile
    n_wide = (n_k - ATTN_NEAR) // ATTN_WIDE
    n_steps = n_wide + ATTN_NEAR
    w0 = jnp.clip((i + 1) // ATTN_WIDE * ATTN_WIDE - ATTN_WIDE, 0, n_k - ATTN_NEAR)
    far_before = w0 // ATTN_WIDE

    def keys_of(t, near):
        if near:
            return pl.multiple_of((w0 + (t - n_wide)) * tile, tile), tile
        first = jnp.where(t < far_before, t, t + ATTN_NEAR // ATTN_WIDE)
        return pl.multiple_of(first * wide, wide), wide

    def scores(t, slot, near):
        start, width = keys_of(t, near)
        k = k_ref[0, pl.ds(start, width), :]
        s = lax.dot_general(q2_ref[...], k, (((1,), (1,)), ((), ())), preferred_element_type=F32)
        if near:
            d = jnp.clip(w0 + (t - n_wide) - i, -BIAS_SPAN, BIAS_SPAN)
            bias = bias_ref[0, d + BIAS_SPAN]
            s = s + jnp.concatenate([bias, bias], axis=0)
        s_ref[slot, :, 0:width] = s
        mc_ref[slot] = jnp.broadcast_to(jnp.max(s, axis=1, keepdims=True), mc_ref.shape[1:])

    def softmax(t, slot, near):
        width = tile if near else wide
        m_prev = m_ref[...]
        if near:
            m_new = jnp.maximum(m_prev, mc_ref[slot])
            shift = m_new
        else:
            const = jnp.where(t < far_before, c_ref[0, 0], c_ref[0, 1])
            m_new = jnp.maximum(m_prev, mc_ref[slot] + const)
            shift = m_new - const
        p = jnp.exp2(s_ref[slot, :, 0:width] - jnp.concatenate([shift] * (width // LANES), axis=1))
        alpha_ref[...] = jnp.exp2(m_prev - m_new)
        m_ref[...] = m_new
        p_ref[slot, :, 0:width] = p.astype(BF16)

    def values(t, slot, near):
        start, width = keys_of(t, near)
        v = v_ref[0, pl.ds(start, width), :]
        v_ones = jnp.concatenate([v, jnp.ones_like(v)], axis=1)
        alpha = jnp.concatenate([alpha_ref[...]] * 2, axis=1)
        acc_ref[...] = alpha * acc_ref[...] + jnp.dot(p_ref[slot, :, 0:width], v_ones,
                                                      preferred_element_type=F32)

    is_near = lambda t: t >= n_wide

    def far_pair(u, carry):
        for parity in range(2):
            t = 2 * u + 1 + parity
            slot = 1 - parity
            values(t - 1, 1 - slot, near=False)
            scores(t + 1, 1 - slot, near=False)
            softmax(t, slot, near=False)
        return carry

    n_pairs = max(n_wide - 2, 0) // 2
    scores(0, 0, is_near(0))
    scores(1, 1, is_near(1))
    softmax(0, 0, is_near(0))
    lax.fori_loop(0, n_pairs, far_pair, 0)
    for t in range(2 * n_pairs + 1, n_steps):
        slot = t % 2
        values(t - 1, 1 - slot, is_near(t - 1))
        if t + 1 < n_steps:
            scores(t + 1, 1 - slot, is_near(t + 1))
        softmax(t, slot, is_near(t))
    values(n_steps - 1, (n_steps - 1) % 2, is_near(n_steps - 1))

    lam = (jnp.exp(jnp.sum(lq1_ref[...] * lk1_ref[...], axis=-1, keepdims=True))
           - jnp.exp(jnp.sum(lq2_ref[...] * lk2_ref[...], axis=-1, keepdims=True))
           + LAMBDA_INIT)
    o1 = acc_ref[0:tile, 0:HEAD_DV] / acc_ref[0:tile, HEAD_DV:]
    o2 = acc_ref[tile:, 0:HEAD_DV] / acc_ref[tile:, HEAD_DV:]
    o = o1 - lam * o2
    o = _rms(o, g_ref[...], SUBLN_EPS) * (1.0 - LAMBDA_INIT)
    o_ref[0] = o.astype(BF16)


def _rel_bucket(rel):
    half = N_BUCKETS // 2
    max_exact = half // 2
    ret = (rel > 0).astype(jnp.int32) * half
    n = jnp.abs(rel)
    nf = jnp.maximum(n, 1).astype(jnp.float32)
    large = max_exact + (jnp.log(nf / max_exact) / math.log(MAX_DISTANCE / max_exact)
                         * (half - max_exact)).astype(jnp.int32)
    large = jnp.minimum(large, half - 1)
    return ret + jnp.where(n < max_exact, n, large)


def _bias_tiles(rel_bias, tile):
    assert tile >= MAX_DISTANCE
    lo = -(BIAS_SPAN + 1) * tile
    rel = jnp.arange(lo, -lo, dtype=jnp.int32)
    by_rel = rel_bias.astype(F32)[_rel_bucket(rel)].T * LOG2E
    consts = jnp.stack([by_rel[:, :1], by_rel[:, -1:]], axis=1)
    consts = jnp.broadcast_to(consts[..., None], (N_HEADS, 2, 1, LANES))
    tiles = []
    for d in range(-BIAS_SPAN, BIAS_SPAN + 1):
        pos = lax.slice_in_dim(by_rel, d * tile - lo, (d + 1) * tile - lo, axis=1)
        neg = lax.slice_in_dim(by_rel, (d - 1) * tile - lo, d * tile - lo, axis=1)
        z = jnp.concatenate([pos, neg], axis=1)
        skew = jnp.tile(z, (1, tile))[:, :tile * (2 * tile - 1)].reshape(-1, tile, 2 * tile - 1)
        tiles.append(skew[:, :, :tile])
    return jnp.stack(tiles, axis=1), consts


def _attn(qkv, bias_tiles, bias_consts, lq1, lk1, lq2, lk2, subln_g):
    b, s, _ = qkv.shape
    tile = ATTN_T
    n_k = s // tile
    assert s % tile == 0 and n_k >= ATTN_NEAR and (n_k - ATTN_NEAR) % ATTN_WIDE == 0
    wide = ATTN_WIDE * tile
    n_bias = 2 * BIAS_SPAN + 1
    k_col = ATTN_QK // LANES
    v_col = 2 * ATTN_QK // LANES
    vec = lambda v: v.reshape(1, -1).astype(F32)
    small = lambda n: pl.BlockSpec((1, n), lambda h, bi, i: (0, 0))
    rows = 2 * tile
    return pl.pallas_call(
        functools.partial(_attn_kernel, tile=tile, n_k=n_k),
        grid=(N_HEADS, b, s // tile),
        in_specs=[
            pl.BlockSpec((1, tile, LANES), lambda h, bi, i: (bi, i, h)),
            pl.BlockSpec((1, s, LANES), lambda h, bi, i: (bi, 0, k_col + h)),
            pl.BlockSpec((1, s, LANES), lambda h, bi, i: (bi, 0, v_col + h)),
            pl.BlockSpec((1, n_bias, tile, tile), lambda h, bi, i: (h, 0, 0, 0)),
            pl.BlockSpec((1, 2, 1, LANES), lambda h, bi, i: (h, 0, 0, 0)),
            small(HEAD_DK), small(HEAD_DK), small(HEAD_DK), small(HEAD_DK), small(HEAD_DV),
        ],
        out_specs=pl.BlockSpec((1, tile, LANES), lambda h, bi, i: (bi, i, h)),
        out_shape=jax.ShapeDtypeStruct((b, s, ATTN_V), BF16),
        scratch_shapes=[
            pltpu.VMEM((rows, LANES), BF16),
            pltpu.VMEM((rows, 2 * HEAD_DV), F32),
            pltpu.VMEM((rows, LANES), F32),
            pltpu.VMEM((rows, LANES), F32),
            pltpu.VMEM((2, rows, wide), F32),
            pltpu.VMEM((2, rows, wide), BF16),
            pltpu.VMEM((2, rows, LANES), F32),
        ],
        compiler_params=pltpu.CompilerParams(
            dimension_semantics=("arbitrary", "arbitrary", "arbitrary"),
            vmem_limit_bytes=VMEM_LIMIT_BYTES),
        name="diff_attn",
    )(qkv, qkv, qkv, bias_tiles, bias_consts, vec(lq1), vec(lk1), vec(lq2), vec(lk2), vec(subln_g))


def _outproj_kernel(x_ref, c_ref, a_ref, wc_ref, wa_ref, o_ref):
    o_ref[...] = (x_ref[...]
                  + jnp.dot(c_ref[...], wc_ref[...], preferred_element_type=F32)
                  + jnp.dot(a_ref[...], wa_ref[...], preferred_element_type=F32))


def _outproj(x, conv_out, att, w_out):
    m, d = x.shape
    tm = PROJ_TM
    return pl.pallas_call(
        _outproj_kernel,
        grid=(m // tm,),
        in_specs=[
            pl.BlockSpec((tm, d), lambda i: (i, 0)),
            pl.BlockSpec((tm, CONV_CH), lambda i: (i, 0)),
            pl.BlockSpec((tm, ATTN_V), lambda i: (i, 0)),
            pl.BlockSpec((CONV_CH, d), lambda i: (0, 0)),
            pl.BlockSpec((ATTN_V, d), lambda i: (1, 0)),
        ],
        out_specs=pl.BlockSpec((tm, d), lambda i: (i, 0)),
        out_shape=jax.ShapeDtypeStruct((m, d), F32),
        compiler_params=pltpu.CompilerParams(
            dimension_semantics=("arbitrary",),
            vmem_limit_bytes=VMEM_LIMIT_BYTES),
        name="outproj",
    )(x, conv_out, att, w_out, w_out)


def _trunk(x, w, bias):
    b, s, d = x.shape
    m = b * s
    x0 = x.reshape(m, d)
    x1 = _ffn(x0, w["ffn1_norm"], w["ffn1_w_in"], w["ffn1_w_out"])
    u, qkv = _inproj(x1, w["mix_norm"], w["w_in"])
    conv_out = _conv(u.reshape(b, s, CONV_CH), w["conv_w"], w["conv_b"], w["conv_ln_g"], w["conv_ln_b"])
    att = _attn(qkv.reshape(b, s, -1), *bias, w["lambda_q1"], w["lambda_k1"],
                w["lambda_q2"], w["lambda_k2"], w["subln_g"])
    x2 = _outproj(x1, conv_out.reshape(m, CONV_CH), att.reshape(m, ATTN_V), w["w_out"])
    y = _ffn(x2, w["ffn2_norm"], w["ffn2_w_in"], w["ffn2_w_out"], final_g=w["final_norm"])
    return y.reshape(b, s, d)


def kernel(x_prompt, x_sample, rel_bias, ffn1_norm, ffn1_w_in, ffn1_w_out, mix_norm, w_in, conv_w, conv_b, conv_ln_g, conv_ln_b, lambda_q1, lambda_k1, lambda_q2, lambda_k2, subln_g, w_out, ffn2_norm, ffn2_w_in, ffn2_w_out, final_norm):
    assert ffn1_norm.shape[0] == 1, "single-layer trunk"
    w = dict(
        ffn1_norm=ffn1_norm[0], ffn1_w_in=ffn1_w_in[0].astype(BF16), ffn1_w_out=ffn1_w_out[0].astype(BF16),
        mix_norm=mix_norm[0], w_in=w_in[0].astype(BF16),
        conv_w=conv_w[0], conv_b=conv_b[0], conv_ln_g=conv_ln_g[0], conv_ln_b=conv_ln_b[0],
        lambda_q1=lambda_q1[0], lambda_k1=lambda_k1[0], lambda_q2=lambda_q2[0], lambda_k2=lambda_k2[0],
        subln_g=subln_g[0], w_out=w_out[0].astype(BF16),
        ffn2_norm=ffn2_norm[0], ffn2_w_in=ffn2_w_in[0].astype(BF16), ffn2_w_out=ffn2_w_out[0].astype(BF16),
        final_norm=final_norm,
    )
    bias = _bias_tiles(rel_bias, ATTN_T)
    return (_trunk(x_prompt, w, bias), _trunk(x_sample, w, bias))
```

```python
import functools
import math

import jax
import jax.numpy as jnp
from jax import lax
from jax.experimental import pallas as pl
from jax.experimental.pallas import tpu as pltpu

F32 = jnp.float32
BF16 = jnp.bfloat16

D_MODEL = 2048
CONV_CH = D_MODEL // 2
CONV_WIDTH = 31
CONV_PAD = CONV_WIDTH // 2
N_HEADS = 8
HEAD_DK = 64
HEAD_DV = 2 * HEAD_DK
ATTN_QK = N_HEADS * 2 * HEAD_DK
ATTN_V = N_HEADS * HEAD_DV
D_FF = 5632
N_BUCKETS = 32
MAX_DISTANCE = 128
RMS_EPS = 1e-6
LN_EPS = 1e-5
SUBLN_EPS = 1e-5
LAYER = 0
LAMBDA_INIT = 0.8 - 0.6 * math.exp(-0.3 * LAYER)
LOG2E = math.log2(math.e)
Q_SCALE = HEAD_DK ** -0.5 * LOG2E

LANES = 128
SUBLANES = 8
VMEM_LIMIT_BYTES = 56 * 1024 * 1024

FFN_TM = 1024
FFN_TF = 512
PROJ_TM = 512
CONV_TS = 512
CONV_HALO = 16
CONV_ROWS = 64
CONV_LANES = 256
ATTN_T = 512
BIAS_SPAN = 2
ATTN_WIDE = 2
ATTN_NEAR = 4


def _rms(x, g, eps):
    return x * lax.rsqrt(jnp.mean(x * x, axis=-1, keepdims=True) + eps) * g


def _ffn_kernel(x_ref, g_ref, wg_ref, wu_ref, wo_ref, *rest, n_steps, final):
    if final:
        fg_ref, o_ref, xn_ref, act_ref = rest
    else:
        o_ref, xn_ref, act_ref = rest
    j = pl.program_id(1)

    def up_stage():
        xn = xn_ref[...]
        gate = jnp.dot(xn, wg_ref[...], preferred_element_type=F32)
        up = jnp.dot(xn, wu_ref[...], preferred_element_type=F32)
        return (gate * jax.nn.sigmoid(gate) * up).astype(BF16)

    def down_stage():
        o_ref[...] += jnp.dot(act_ref[...], wo_ref[...], preferred_element_type=F32)

    @pl.when(j == 0)
    def _():
        xn_ref[...] = _rms(x_ref[...], g_ref[...], RMS_EPS).astype(BF16)
        o_ref[...] = jnp.zeros_like(o_ref)
        act_ref[...] = up_stage()

    @pl.when((j > 0) & (j < n_steps))
    def _():
        down_stage()
        act_ref[...] = up_stage()

    @pl.when(j == n_steps)
    def _():
        down_stage()
        y = x_ref[...] + 0.5 * o_ref[...]
        if final:
            y = _rms(y, fg_ref[...], RMS_EPS)
        o_ref[...] = y


def _ffn(x, norm_g, w_in, w_out, final_g=None):
    m, d = x.shape
    tm, tf = FFN_TM, FFN_TF
    n_steps = D_FF // tf
    last = n_steps - 1
    final = final_g is not None
    in_specs = [
        pl.BlockSpec((tm, d), lambda i, j: (i, 0), pipeline_mode=pl.Buffered(1)),
        pl.BlockSpec((1, d), lambda i, j: (0, 0)),
        pl.BlockSpec((d, tf), lambda i, j: (0, jnp.minimum(j, last))),
        pl.BlockSpec((d, tf), lambda i, j: (0, jnp.minimum(j, last) + n_steps)),
        pl.BlockSpec((tf, d), lambda i, j: (jnp.maximum(j - 1, 0), 0)),
    ]
    args = [x, norm_g.reshape(1, d), w_in, w_in, w_out]
    if final:
        in_specs.append(pl.BlockSpec((1, d), lambda i, j: (0, 0)))
        args.append(final_g.reshape(1, d))
    return pl.pallas_call(
        functools.partial(_ffn_kernel, n_steps=n_steps, final=final),
        grid=(m // tm, n_steps + 1),
        in_specs=in_specs,
        out_specs=pl.BlockSpec((tm, d), lambda i, j: (i, 0)),
        out_shape=jax.ShapeDtypeStruct((m, d), F32),
        scratch_shapes=[pltpu.VMEM((tm, d), BF16), pltpu.VMEM((tm, tf), BF16)],
        compiler_params=pltpu.CompilerParams(
            dimension_semantics=("arbitrary", "arbitrary"),
            vmem_limit_bytes=VMEM_LIMIT_BYTES),
        name="ffn_final" if final else "ffn",
    )(*args)


def _inproj_kernel(x_ref, g_ref, wa_ref, wgate_ref, wqkv_ref, u_ref, qkv_ref):
    xn = _rms(x_ref[...], g_ref[...], RMS_EPS).astype(BF16)
    a = jnp.dot(xn, wa_ref[...], preferred_element_type=F32)
    gate = jnp.dot(xn, wgate_ref[...], preferred_element_type=F32)
    u_ref[...] = a * jax.nn.sigmoid(gate)
    for c, scale in enumerate((Q_SCALE, None, None)):
        cols = slice(c * ATTN_QK, (c + 1) * ATTN_QK)
        h = jnp.dot(xn, wqkv_ref[:, cols], preferred_element_type=F32)
        qkv_ref[:, cols] = (h if scale is None else h * scale).astype(BF16)


def _inproj(x, norm_g, w_in):
    m, d = x.shape
    tm = PROJ_TM
    n_qkv = 2 * ATTN_QK + ATTN_V
    return pl.pallas_call(
        _inproj_kernel,
        grid=(m // tm,),
        in_specs=[
            pl.BlockSpec((tm, d), lambda i: (i, 0)),
            pl.BlockSpec((1, d), lambda i: (0, 0)),
            pl.BlockSpec((d, CONV_CH), lambda i: (0, 0)),
            pl.BlockSpec((d, CONV_CH), lambda i: (0, 1)),
            pl.BlockSpec((d, n_qkv), lambda i: (0, 0)),
        ],
        out_specs=[
            pl.BlockSpec((tm, CONV_CH), lambda i: (i, 0)),
            pl.BlockSpec((tm, n_qkv), lambda i: (i, 0)),
        ],
        out_shape=[
            jax.ShapeDtypeStruct((m, CONV_CH), F32),
            jax.ShapeDtypeStruct((m, n_qkv), BF16),
        ],
        compiler_params=pltpu.CompilerParams(
            dimension_semantics=("arbitrary",),
            vmem_limit_bytes=VMEM_LIMIT_BYTES),
        name="inproj",
    )(x, norm_g.reshape(1, d), w_in, w_in, w_in[:, 2 * CONV_CH:])


def _conv_kernel(prev_ref, cur_ref, next_ref, w_ref, cb_ref, lg_ref, lb_ref, o_ref, buf_ref,
                 *, ts, n_blocks):
    i = pl.program_id(1)
    halo = CONV_HALO
    prev = prev_ref[0]
    nxt = next_ref[0]
    buf_ref[0:halo, :] = jnp.where(i == 0, jnp.zeros_like(prev), prev)
    buf_ref[halo:halo + ts, :] = cur_ref[0]
    buf_ref[halo + ts:, :] = jnp.where(i == n_blocks - 1, jnp.zeros_like(nxt), nxt)

    rows = CONV_ROWS
    first = halo - CONV_PAD
    for c in range(ts // rows):
        r0 = c * rows
        parts = []
        for g in range(CONV_CH // CONV_LANES):
            lanes = slice(g * CONV_LANES, (g + 1) * CONV_LANES)
            y = None
            for b in range(SUBLANES):
                zb = None
                for a in range((CONV_WIDTH - b + SUBLANES - 1) // SUBLANES):
                    t = SUBLANES * a + b
                    lo = r0 + SUBLANES * a
                    w_tap = jnp.concatenate([w_ref[t, :, lanes]] * (rows // SUBLANES + 1), axis=0)
                    term = buf_ref[lo:lo + rows + SUBLANES, lanes] * w_tap
                    zb = term if zb is None else zb + term
                k = first + b
                if k % SUBLANES:
                    zb = pltpu.roll(zb, rows + SUBLANES - k, 0)
                    k = 0
                shifted = zb[k:k + rows, :]
                y = shifted if y is None else y + shifted
            parts.append(y)
        y = jnp.concatenate(parts, axis=1) + cb_ref[...]
        mu = jnp.mean(y, axis=-1, keepdims=True)
        yc = y - mu
        var = jnp.mean(yc * yc, axis=-1, keepdims=True)
        z = yc * lax.rsqrt(var + LN_EPS) * lg_ref[...] + lb_ref[...]
        o_ref[0, r0:r0 + rows, :] = (z * jax.nn.sigmoid(z)).astype(BF16)


def _conv(u, conv_w, conv_b, ln_g, ln_b):
    b, s, c = u.shape
    ts = CONV_TS
    n_blocks = s // ts
    per = ts // CONV_HALO
    n_halo = s // CONV_HALO
    row = lambda v: v.reshape(1, c)
    return pl.pallas_call(
        functools.partial(_conv_kernel, ts=ts, n_blocks=n_blocks),
        grid=(b, n_blocks),
        in_specs=[
            pl.BlockSpec((1, CONV_HALO, c), lambda bi, i: (bi, jnp.maximum(i * per - 1, 0), 0)),
            pl.BlockSpec((1, ts, c), lambda bi, i: (bi, i, 0)),
            pl.BlockSpec((1, CONV_HALO, c), lambda bi, i: (bi, jnp.minimum((i + 1) * per, n_halo - 1), 0)),
            pl.BlockSpec((CONV_WIDTH, SUBLANES, c), lambda bi, i: (0, 0, 0)),
            pl.BlockSpec((1, c), lambda bi, i: (0, 0)),
            pl.BlockSpec((1, c), lambda bi, i: (0, 0)),
            pl.BlockSpec((1, c), lambda bi, i: (0, 0)),
        ],
        out_specs=pl.BlockSpec((1, ts, c), lambda bi, i: (bi, i, 0)),
        out_shape=jax.ShapeDtypeStruct((b, s, c), BF16),
        scratch_shapes=[pltpu.VMEM((ts + 2 * CONV_HALO, c), F32)],
        compiler_params=pltpu.CompilerParams(
            dimension_semantics=("arbitrary", "arbitrary"),
            vmem_limit_bytes=VMEM_LIMIT_BYTES),
        name="conv",
    )(u, u, u, jnp.broadcast_to(conv_w[:, None, :], (CONV_WIDTH, SUBLANES, c)),
      row(conv_b), row(ln_g), row(ln_b))


def _attn_kernel(q_ref, k_ref, v_ref, bias_ref, c_ref, lq1_ref, lk1_ref, lq2_ref, lk2_ref, g_ref, o_ref,
                 q2_ref, acc_ref, m_ref, alpha_ref, s_ref, p_ref, mc_ref, *, tile, n_k):
    i = pl.program_id(2)
    q = q_ref[0]
    lane = lax.broadcasted_iota(jnp.int32, q.shape, 1)
    zero = jnp.zeros_like(q)
    q2_ref[0:tile, :] = jnp.where(lane < HEAD_DK, q, zero)
    q2_ref[tile:, :] = jnp.where(lane >= HEAD_DK, q, zero)
    acc_ref[...] = jnp.zeros_like(acc_ref)
    m_ref[...] = jnp.full_like(m_ref, -jnp.inf)

    wide = ATTN_WIDE * tile
    n_wide = (n_k - ATTN_NEAR) // ATTN_WIDE
    n_steps = n_wide + ATTN_NEAR
    w0 = jnp.clip((i + 1) // ATTN_WIDE * ATTN_WIDE - ATTN_WIDE, 0, n_k - ATTN_NEAR)
    far_before = w0 // ATTN_WIDE

    def keys_of(t, near):
        if near:
            return pl.multiple_of((w0 + (t - n_wide)) * tile, tile), tile
        first = jnp.where(t < far_before, t, t + ATTN_NEAR // ATTN_WIDE)
        return pl.multiple_of(first * wide, wide), wide

    def scores(t, slot, near):
        start, width = keys_of(t, near)
        k = k_ref[0, pl.ds(start, width), :]
        s = lax.dot_general(q2_ref[...], k, (((1,), (1,)), ((), ())), preferred_element_type=F32)
        if near:
            d = jnp.clip(w0 + (t - n_wide) - i, -BIAS_SPAN, BIAS_SPAN)
            bias = bias_ref[0, d + BIAS_SPAN]
            s = s + jnp.concatenate([bias, bias], axis=0)
        s_ref[slot, :, 0:width] = s
        mc_ref[slot] = jnp.broadcast_to(jnp.max(s, axis=1, keepdims=True), mc_ref.shape[1:])

    def softmax(t, slot, near):
        width = tile if near else wide
        m_prev = m_ref[...]
        if near:
            m_new = jnp.maximum(m_prev, mc_ref[slot])
            shift = m_new
        else:
            const = jnp.where(t < far_before, c_ref[0, 0], c_ref[0, 1])
            m_new = jnp.maximum(m_prev, mc_ref[slot] + const)
            shift = m_new - const
        p = jnp.exp2(s_ref[slot, :, 0:width] - jnp.concatenate([shift] * (width // LANES), axis=1))
        alpha_ref[...] = jnp.exp2(m_prev - m_new)
        m_ref[...] = m_new
        p_ref[slot, :, 0:width] = p.astype(BF16)

    def values(t, slot, near):
        start, width = keys_of(t, near)
        v = v_ref[0, pl.ds(start, width), :]
        v_ones = jnp.concatenate([v, jnp.ones_like(v)], axis=1)
        alpha = jnp.concatenate([alpha_ref[...]] * 2, axis=1)
        acc_ref[...] = alpha * acc_ref[...] + jnp.dot(p_ref[slot, :, 0:width], v_ones,
                                                      preferred_element_type=F32)

    is_near = lambda t: t >= n_wide

    def far_pair(u, carry):
        for parity in range(2):
            t = 2 * u + 1 + parity
            slot = 1 - parity
            values(t - 1, 1 - slot, near=False)
            scores(t + 1, 1 - slot, near=False)
            softmax(t, slot, near=False)
        return carry

    n_pairs = max(n_wide - 2, 0) // 2
    scores(0, 0, is_near(0))
    scores(1, 1, is_near(1))
    softmax(0, 0, is_near(0))
    lax.fori_loop(0, n_pairs, far_pair, 0)
    for t in range(2 * n_pairs + 1, n_steps):
        slot = t % 2
        values(t - 1, 1 - slot, is_near(t - 1))
        if t + 1 < n_steps:
            scores(t + 1, 1 - slot, is_near(t + 1))
        softmax(t, slot, is_near(t))
    values(n_steps - 1, (n_steps - 1) % 2, is_near(n_steps - 1))

    lam = (jnp.exp(jnp.sum(lq1_ref[...] * lk1_ref[...], axis=-1, keepdims=True))
           - jnp.exp(jnp.sum(lq2_ref[...] * lk2_ref[...], axis=-1, keepdims=True))
           + LAMBDA_INIT)
    o1 = acc_ref[0:tile, 0:HEAD_DV] / acc_ref[0:tile, HEAD_DV:]
    o2 = acc_ref[tile:, 0:HEAD_DV] / acc_ref[tile:, HEAD_DV:]
    o = o1 - lam * o2
    o = _rms(o, g_ref[...], SUBLN_EPS) * (1.0 - LAMBDA_INIT)
    o_ref[0] = o.astype(BF16)


def _rel_bucket(rel):
    half = N_BUCKETS // 2
    max_exact = half // 2
    ret = (rel > 0).astype(jnp.int32) * half
    n = jnp.abs(rel)
    nf = jnp.maximum(n, 1).astype(jnp.float32)
    large = max_exact + (jnp.log(nf / max_exact) / math.log(MAX_DISTANCE / max_exact)
                         * (half - max_exact)).astype(jnp.int32)
    large = jnp.minimum(large, half - 1)
    return ret + jnp.where(n < max_exact, n, large)


def _bias_tiles(rel_bias, tile):
    assert tile >= MAX_DISTANCE
    lo = -(BIAS_SPAN + 1) * tile
    rel = jnp.arange(lo, -lo, dtype=jnp.int32)
    by_rel = rel_bias.astype(F32)[_rel_bucket(rel)].T * LOG2E
    consts = jnp.stack([by_rel[:, :1], by_rel[:, -1:]], axis=1)
    consts = jnp.broadcast_to(consts[..., None], (N_HEADS, 2, 1, LANES))
    tiles = []
    for d in range(-BIAS_SPAN, BIAS_SPAN + 1):
        pos = lax.slice_in_dim(by_rel, d * tile - lo, (d + 1) * tile - lo, axis=1)
        neg = lax.slice_in_dim(by_rel, (d - 1) * tile - lo, d * tile - lo, axis=1)
        z = jnp.concatenate([pos, neg], axis=1)
        skew = jnp.tile(z, (1, tile))[:, :tile * (2 * tile - 1)].reshape(-1, tile, 2 * tile - 1)
        tiles.append(skew[:, :, :tile])
    return jnp.stack(tiles, axis=1), consts


def _attn(qkv, bias_tiles, bias_consts, lq1, lk1, lq2, lk2, subln_g):
    b, s, _ = qkv.shape
    tile = ATTN_T
    n_k = s // tile
    assert s % tile == 0 and n_k >= ATTN_NEAR and (n_k - ATTN_NEAR) % ATTN_WIDE == 0
    wide = ATTN_WIDE * tile
    n_bias = 2 * BIAS_SPAN + 1
    k_col = ATTN_QK // LANES
    v_col = 2 * ATTN_QK // LANES
    vec = lambda v: v.reshape(1, -1).astype(F32)
    small = lambda n: pl.BlockSpec((1, n), lambda h, bi, i: (0, 0))
    rows = 2 * tile
    return pl.pallas_call(
        functools.partial(_attn_kernel, tile=tile, n_k=n_k),
        grid=(N_HEADS, b, s // tile),
        in_specs=[
            pl.BlockSpec((1, tile, LANES), lambda h, bi, i: (bi, i, h)),
            pl.BlockSpec((1, s, LANES), lambda h, bi, i: (bi, 0, k_col + h)),
            pl.BlockSpec((1, s, LANES), lambda h, bi, i: (bi, 0, v_col + h)),
            pl.BlockSpec((1, n_bias, tile, tile), lambda h, bi, i: (h, 0, 0, 0)),
            pl.BlockSpec((1, 2, 1, LANES), lambda h, bi, i: (h, 0, 0, 0)),
            small(HEAD_DK), small(HEAD_DK), small(HEAD_DK), small(HEAD_DK), small(HEAD_DV),
        ],
        out_specs=pl.BlockSpec((1, tile, LANES), lambda h, bi, i: (bi, i, h)),
        out_shape=jax.ShapeDtypeStruct((b, s, ATTN_V), BF16),
        scratch_shapes=[
            pltpu.VMEM((rows, LANES), BF16),
            pltpu.VMEM((rows, 2 * HEAD_DV), F32),
            pltpu.VMEM((rows, LANES), F32),
            pltpu.VMEM((rows, LANES), F32),
            pltpu.VMEM((2, rows, wide), F32),
            pltpu.VMEM((2, rows, wide), BF16),
            pltpu.VMEM((2, rows, LANES), F32),
        ],
        compiler_params=pltpu.CompilerParams(
            dimension_semantics=("arbitrary", "arbitrary", "arbitrary"),
            vmem_limit_bytes=VMEM_LIMIT_BYTES),
        name="diff_attn",
    )(qkv, qkv, qkv, bias_tiles, bias_consts, vec(lq1), vec(lk1), vec(lq2), vec(lk2), vec(subln_g))


def _outproj_kernel(x_ref, c_ref, a_ref, wc_ref, wa_ref, o_ref):
    o_ref[...] = (x_ref[...]
                  + jnp.dot(c_ref[...], wc_ref[...], preferred_element_type=F32)
                  + jnp.dot(a_ref[...], wa_ref[...], preferred_element_type=F32))


def _outproj(x, conv_out, att, w_out):
    m, d = x.shape
    tm = PROJ_TM
    return pl.pallas_call(
        _outproj_kernel,
        grid=(m // tm,),
        in_specs=[
            pl.BlockSpec((tm, d), lambda i: (i, 0)),
            pl.BlockSpec((tm, CONV_CH), lambda i: (i, 0)),
            pl.BlockSpec((tm, ATTN_V), lambda i: (i, 0)),
            pl.BlockSpec((CONV_CH, d), lambda i: (0, 0)),
            pl.BlockSpec((ATTN_V, d), lambda i: (1, 0)),
        ],
        out_specs=pl.BlockSpec((tm, d), lambda i: (i, 0)),
        out_shape=jax.ShapeDtypeStruct((m, d), F32),
        compiler_params=pltpu.CompilerParams(
            dimension_semantics=("arbitrary",),
            vmem_limit_bytes=VMEM_LIMIT_BYTES),
        name="outproj",
    )(x, conv_out, att, w_out, w_out)


def _trunk(x, w, bias):
    b, s, d = x.shape
    m = b * s
    x0 = x.reshape(m, d)
    x1 = _ffn(x0, w["ffn1_norm"], w["ffn1_w_in"], w["ffn1_w_out"])
    u, qkv = _inproj(x1, w["mix_norm"], w["w_in"])
    conv_out = _conv(u.reshape(b, s, CONV_CH), w["conv_w"], w["conv_b"], w["conv_ln_g"], w["conv_ln_b"])
    att = _attn(qkv.reshape(b, s, -1), *bias, w["lambda_q1"], w["lambda_k1"],
                w["lambda_q2"], w["lambda_k2"], w["subln_g"])
    x2 = _outproj(x1, conv_out.reshape(m, CONV_CH), att.reshape(m, ATTN_V), w["w_out"])
    y = _ffn(x2, w["ffn2_norm"], w["ffn2_w_in"], w["ffn2_w_out"], final_g=w["final_norm"])
    return y.reshape(b, s, d)


def kernel(x_prompt, x_sample, rel_bias, ffn1_norm, ffn1_w_in, ffn1_w_out, mix_norm, w_in, conv_w, conv_b, conv_ln_g, conv_ln_b, lambda_q1, lambda_k1, lambda_q2, lambda_k2, subln_g, w_out, ffn2_norm, ffn2_w_in, ffn2_w_out, final_norm):
    assert ffn1_norm.shape[0] == 1, "single-layer trunk"
    w = dict(
        ffn1_norm=ffn1_norm[0], ffn1_w_in=ffn1_w_in[0].astype(BF16), ffn1_w_out=ffn1_w_out[0].astype(BF16),
        mix_norm=mix_norm[0], w_in=w_in[0].astype(BF16),
        conv_w=conv_w[0], conv_b=conv_b[0], conv_ln_g=conv_ln_g[0], conv_ln_b=conv_ln_b[0],
        lambda_q1=lambda_q1[0], lambda_k1=lambda_k1[0], lambda_q2=lambda_q2[0], lambda_k2=lambda_k2[0],
        subln_g=subln_g[0], w_out=w_out[0].astype(BF16),
        ffn2_norm=ffn2_norm[0], ffn2_w_in=ffn2_w_in[0].astype(BF16), ffn2_w_out=ffn2_w_out[0].astype(BF16),
        final_norm=final_norm,
    )
    bias = _bias_tiles(rel_bias, ATTN_T)
    return (_trunk(x_prompt, w, bias), _trunk(x_sample, w, bias))
```

```python
import functools
import math

import jax
import jax.numpy as jnp
from jax import lax
from jax.experimental import pallas as pl
from jax.experimental.pallas import tpu as pltpu

F32 = jnp.float32
BF16 = jnp.bfloat16

D_MODEL = 2048
CONV_CH = D_MODEL // 2
CONV_WIDTH = 31
CONV_PAD = CONV_WIDTH // 2
N_HEADS = 8
HEAD_DK = 64
HEAD_DV = 2 * HEAD_DK
ATTN_QK = N_HEADS * 2 * HEAD_DK
ATTN_V = N_HEADS * HEAD_DV
D_FF = 5632
N_BUCKETS = 32
MAX_DISTANCE = 128
RMS_EPS = 1e-6
LN_EPS = 1e-5
SUBLN_EPS = 1e-5
LAYER = 0
LAMBDA_INIT = 0.8 - 0.6 * math.exp(-0.3 * LAYER)
LOG2E = math.log2(math.e)
Q_SCALE = HEAD_DK ** -0.5 * LOG2E

LANES = 128
SUBLANES = 8
VMEM_LIMIT_BYTES = 56 * 1024 * 1024

FFN_TM = 512
FFN_TF = 512
PROJ_TM = 512
CONV_TS = 512
CONV_HALO = 16
CONV_ROWS = 64
CONV_LANES = 256
ATTN_T = 512
BIAS_SPAN = 2
ATTN_WIDE = 2
ATTN_NEAR = 4


def _rms(x, g, eps):
    return x * lax.rsqrt(jnp.mean(x * x, axis=-1, keepdims=True) + eps) * g


def _ffn_kernel(x_ref, g_ref, wg_ref, wu_ref, wo_ref, *rest, n_steps, final):
    if final:
        fg_ref, o_ref, xn_ref, act_ref, acc_ref = rest
    else:
        o_ref, xn_ref, act_ref, acc_ref = rest
    j = pl.program_id(1)

    def up_stage():
        xn = xn_ref[...]
        gate = jnp.dot(xn, wg_ref[...], preferred_element_type=F32)
        up = jnp.dot(xn, wu_ref[...], preferred_element_type=F32)
        return (gate * jax.nn.sigmoid(gate) * up).astype(BF16)

    def down_stage():
        acc_ref[...] += jnp.dot(act_ref[...], wo_ref[...], preferred_element_type=F32)

    @pl.when(j == 0)
    def _():
        xn_ref[...] = _rms(x_ref[...], g_ref[...], RMS_EPS).astype(BF16)
        acc_ref[...] = jnp.zeros_like(acc_ref)
        act_ref[...] = up_stage()

    @pl.when((j > 0) & (j < n_steps))
    def _():
        down_stage()
        act_ref[...] = up_stage()

    @pl.when(j == n_steps)
    def _():
        down_stage()
        y = x_ref[...] + 0.5 * acc_ref[...]
        if final:
            y = _rms(y, fg_ref[...], RMS_EPS)
        o_ref[...] = y


def _ffn(x, norm_g, w_in, w_out, final_g=None):
    m, d = x.shape
    tm, tf = FFN_TM, FFN_TF
    n_steps = D_FF // tf
    last = n_steps - 1
    final = final_g is not None
    in_specs = [
        pl.BlockSpec((tm, d), lambda i, j: (i, 0)),
        pl.BlockSpec((1, d), lambda i, j: (0, 0)),
        pl.BlockSpec((d, tf), lambda i, j: (0, jnp.minimum(j, last))),
        pl.BlockSpec((d, tf), lambda i, j: (0, jnp.minimum(j, last) + n_steps)),
        pl.BlockSpec((tf, d), lambda i, j: (jnp.maximum(j - 1, 0), 0)),
    ]
    args = [x, norm_g.reshape(1, d), w_in, w_in, w_out]
    if final:
        in_specs.append(pl.BlockSpec((1, d), lambda i, j: (0, 0)))
        args.append(final_g.reshape(1, d))
    return pl.pallas_call(
        functools.partial(_ffn_kernel, n_steps=n_steps, final=final),
        grid=(m // tm, n_steps + 1),
        in_specs=in_specs,
        out_specs=pl.BlockSpec((tm, d), lambda i, j: (i, 0)),
        out_shape=jax.ShapeDtypeStruct((m, d), F32),
        scratch_shapes=[pltpu.VMEM((tm, d), BF16), pltpu.VMEM((tm, tf), BF16), pltpu.VMEM((tm, d), F32)],
        compiler_params=pltpu.CompilerParams(
            dimension_semantics=("arbitrary", "arbitrary"),
            vmem_limit_bytes=VMEM_LIMIT_BYTES),
        name="ffn_final" if final else "ffn",
    )(*args)


def _inproj_kernel(x_ref, g_ref, wa_ref, wgate_ref, wqkv_ref, u_ref, qkv_ref):
    xn = _rms(x_ref[...], g_ref[...], RMS_EPS).astype(BF16)
    a = jnp.dot(xn, wa_ref[...], preferred_element_type=F32)
    gate = jnp.dot(xn, wgate_ref[...], preferred_element_type=F32)
    u_ref[...] = a * jax.nn.sigmoid(gate)
    for c, scale in enumerate((Q_SCALE, None, None)):
        cols = slice(c * ATTN_QK, (c + 1) * ATTN_QK)
        h = jnp.dot(xn, wqkv_ref[:, cols], preferred_element_type=F32)
        qkv_ref[:, cols] = (h if scale is None else h * scale).astype(BF16)


def _inproj(x, norm_g, w_in):
    m, d = x.shape
    tm = PROJ_TM
    n_qkv = 2 * ATTN_QK + ATTN_V
    return pl.pallas_call(
        _inproj_kernel,
        grid=(m // tm,),
        in_specs=[
            pl.BlockSpec((tm, d), lambda i: (i, 0)),
            pl.BlockSpec((1, d), lambda i: (0, 0)),
            pl.BlockSpec((d, CONV_CH), lambda i: (0, 0)),
            pl.BlockSpec((d, CONV_CH), lambda i: (0, 1)),
            pl.BlockSpec((d, n_qkv), lambda i: (0, 0)),
        ],
        out_specs=[
            pl.BlockSpec((tm, CONV_CH), lambda i: (i, 0)),
            pl.BlockSpec((tm, n_qkv), lambda i: (i, 0)),
        ],
        out_shape=[
            jax.ShapeDtypeStruct((m, CONV_CH), F32),
            jax.ShapeDtypeStruct((m, n_qkv), BF16),
        ],
        compiler_params=pltpu.CompilerParams(
            dimension_semantics=("arbitrary",),
            vmem_limit_bytes=VMEM_LIMIT_BYTES),
        name="inproj",
    )(x, norm_g.reshape(1, d), w_in, w_in, w_in[:, 2 * CONV_CH:])


def _conv_kernel(prev_ref, cur_ref, next_ref, w_ref, cb_ref, lg_ref, lb_ref, o_ref, buf_ref,
                 *, ts, n_blocks):
    i = pl.program_id(1)
    halo = CONV_HALO
    prev = prev_ref[0]
    nxt = next_ref[0]
    buf_ref[0:halo, :] = jnp.where(i == 0, jnp.zeros_like(prev), prev)
    buf_ref[halo:halo + ts, :] = cur_ref[0]
    buf_ref[halo + ts:, :] = jnp.where(i == n_blocks - 1, jnp.zeros_like(nxt), nxt)

    rows = CONV_ROWS
    first = halo - CONV_PAD
    for c in range(ts // rows):
        r0 = c * rows
        parts = []
        for g in range(CONV_CH // CONV_LANES):
            lanes = slice(g * CONV_LANES, (g + 1) * CONV_LANES)
            y = None
            for b in range(SUBLANES):
                zb = None
                for a in range((CONV_WIDTH - b + SUBLANES - 1) // SUBLANES):
                    t = SUBLANES * a + b
                    lo = r0 + SUBLANES * a
                    w_tap = jnp.concatenate([w_ref[t, :, lanes]] * (rows // SUBLANES + 1), axis=0)
                    term = buf_ref[lo:lo + rows + SUBLANES, lanes] * w_tap
                    zb = term if zb is None else zb + term
                k = first + b
                if k % SUBLANES:
                    zb = pltpu.roll(zb, rows + SUBLANES - k, 0)
                    k = 0
                shifted = zb[k:k + rows, :]
                y = shifted if y is None else y + shifted
            parts.append(y)
        y = jnp.concatenate(parts, axis=1) + cb_ref[...]
        mu = jnp.mean(y, axis=-1, keepdims=True)
        yc = y - mu
        var = jnp.mean(yc * yc, axis=-1, keepdims=True)
        z = yc * lax.rsqrt(var + LN_EPS) * lg_ref[...] + lb_ref[...]
        o_ref[0, r0:r0 + rows, :] = (z * jax.nn.sigmoid(z)).astype(BF16)


def _conv(u, conv_w, conv_b, ln_g, ln_b):
    b, s, c = u.shape
    ts = CONV_TS
    n_blocks = s // ts
    per = ts // CONV_HALO
    n_halo = s // CONV_HALO
    row = lambda v: v.reshape(1, c)
    return pl.pallas_call(
        functools.partial(_conv_kernel, ts=ts, n_blocks=n_blocks),
        grid=(b, n_blocks),
        in_specs=[
            pl.BlockSpec((1, CONV_HALO, c), lambda bi, i: (bi, jnp.maximum(i * per - 1, 0), 0)),
            pl.BlockSpec((1, ts, c), lambda bi, i: (bi, i, 0)),
            pl.BlockSpec((1, CONV_HALO, c), lambda bi, i: (bi, jnp.minimum((i + 1) * per, n_halo - 1), 0)),
            pl.BlockSpec((CONV_WIDTH, SUBLANES, c), lambda bi, i: (0, 0, 0)),
            pl.BlockSpec((1, c), lambda bi, i: (0, 0)),
            pl.BlockSpec((1, c), lambda bi, i: (0, 0)),
            pl.BlockSpec((1, c), lambda bi, i: (0, 0)),
        ],
        out_specs=pl.BlockSpec((1, ts, c), lambda bi, i: (bi, i, 0)),
        out_shape=jax.ShapeDtypeStruct((b, s, c), BF16),
        scratch_shapes=[pltpu.VMEM((ts + 2 * CONV_HALO, c), F32)],
        compiler_params=pltpu.CompilerParams(
            dimension_semantics=("arbitrary", "arbitrary"),
            vmem_limit_bytes=VMEM_LIMIT_BYTES),
        name="conv",
    )(u, u, u, jnp.broadcast_to(conv_w[:, None, :], (CONV_WIDTH, SUBLANES, c)),
      row(conv_b), row(ln_g), row(ln_b))


def _attn_kernel(q_ref, k_ref, v_ref, bias_ref, c_ref, lq1_ref, lk1_ref, lq2_ref, lk2_ref, g_ref, o_ref,
                 q2_ref, acc_ref, m_ref, alpha_ref, s_ref, p_ref, mc_ref, *, tile, n_k):
    i = pl.program_id(2)
    q = q_ref[0]
    lane = lax.broadcasted_iota(jnp.int32, q.shape, 1)
    zero = jnp.zeros_like(q)
    q2_ref[0:tile, :] = jnp.where(lane < HEAD_DK, q, zero)
    q2_ref[tile:, :] = jnp.where(lane >= HEAD_DK, q, zero)
    acc_ref[...] = jnp.zeros_like(acc_ref)
    m_ref[...] = jnp.full_like(m_ref, -jnp.inf)

    wide = ATTN_WIDE * tile
    n_wide = (n_k - ATTN_NEAR) // ATTN_WIDE
    n_steps = n_wide + ATTN_NEAR
    w0 = jnp.clip((i + 1) // ATTN_WIDE * ATTN_WIDE - ATTN_WIDE, 0, n_k - ATTN_NEAR)
    far_before = w0 // ATTN_WIDE

    def keys_of(t, near):
        if near:
            return pl.multiple_of((w0 + (t - n_wide)) * tile, tile), tile
        first = jnp.where(t < far_before, t, t + ATTN_NEAR // ATTN_WIDE)
        return pl.multiple_of(first * wide, wide), wide

    def scores(t, slot, near):
        start, width = keys_of(t, near)
        k = k_ref[0, pl.ds(start, width), :]
        s = lax.dot_general(q2_ref[...], k, (((1,), (1,)), ((), ())), preferred_element_type=F32)
        if near:
            d = jnp.clip(w0 + (t - n_wide) - i, -BIAS_SPAN, BIAS_SPAN)
            bias = bias_ref[0, d + BIAS_SPAN]
            s = s + jnp.concatenate([bias, bias], axis=0)
        s_ref[slot, :, 0:width] = s
        mc_ref[slot] = jnp.broadcast_to(jnp.max(s, axis=1, keepdims=True), mc_ref.shape[1:])

    def softmax(t, slot, near):
        width = tile if near else wide
        m_prev = m_ref[...]
        if near:
            m_new = jnp.maximum(m_prev, mc_ref[slot])
            shift = m_new
        else:
            const = jnp.where(t < far_before, c_ref[0, 0], c_ref[0, 1])
            m_new = jnp.maximum(m_prev, mc_ref[slot] + const)
            shift = m_new - const
        p = jnp.exp2(s_ref[slot, :, 0:width] - jnp.concatenate([shift] * (width // LANES), axis=1))
        alpha_ref[...] = jnp.exp2(m_prev - m_new)
        m_ref[...] = m_new
        p_ref[slot, :, 0:width] = p.astype(BF16)

    def values(t, slot, near):
        start, width = keys_of(t, near)
        v = v_ref[0, pl.ds(start, width), :]
        v_ones = jnp.concatenate([v, jnp.ones_like(v)], axis=1)
        alpha = jnp.concatenate([alpha_ref[...]] * 2, axis=1)
        acc_ref[...] = alpha * acc_ref[...] + jnp.dot(p_ref[slot, :, 0:width], v_ones,
                                                      preferred_element_type=F32)

    is_near = lambda t: t >= n_wide

    def far_pair(u, carry):
        for parity in range(2):
            t = 2 * u + 1 + parity
            slot = 1 - parity
            values(t - 1, 1 - slot, near=False)
            scores(t + 1, 1 - slot, near=False)
            softmax(t, slot, near=False)
        return carry

    n_pairs = max(n_wide - 2, 0) // 2
    scores(0, 0, is_near(0))
    scores(1, 1, is_near(1))
    softmax(0, 0, is_near(0))
    lax.fori_loop(0, n_pairs, far_pair, 0)
    for t in range(2 * n_pairs + 1, n_steps):
        slot = t % 2
        values(t - 1, 1 - slot, is_near(t - 1))
        if t + 1 < n_steps:
            scores(t + 1, 1 - slot, is_near(t + 1))
        softmax(t, slot, is_near(t))
    values(n_steps - 1, (n_steps - 1) % 2, is_near(n_steps - 1))

    lam = (jnp.exp(jnp.sum(lq1_ref[...] * lk1_ref[...], axis=-1, keepdims=True))
           - jnp.exp(jnp.sum(lq2_ref[...] * lk2_ref[...], axis=-1, keepdims=True))
           + LAMBDA_INIT)
    o1 = acc_ref[0:tile, 0:HEAD_DV] / acc_ref[0:tile, HEAD_DV:]
    o2 = acc_ref[tile:, 0:HEAD_DV] / acc_ref[tile:, HEAD_DV:]
    o = o1 - lam * o2
    o = _rms(o, g_ref[...], SUBLN_EPS) * (1.0 - LAMBDA_INIT)
    o_ref[0] = o.astype(BF16)


def _rel_bucket(rel):
    half = N_BUCKETS // 2
    max_exact = half // 2
    ret = (rel > 0).astype(jnp.int32) * half
    n = jnp.abs(rel)
    nf = jnp.maximum(n, 1).astype(jnp.float32)
    large = max_exact + (jnp.log(nf / max_exact) / math.log(MAX_DISTANCE / max_exact)
                         * (half - max_exact)).astype(jnp.int32)
    large = jnp.minimum(large, half - 1)
    return ret + jnp.where(n < max_exact, n, large)


def _bias_tiles(rel_bias, tile):
    assert tile >= MAX_DISTANCE
    lo = -(BIAS_SPAN + 1) * tile
    rel = jnp.arange(lo, -lo, dtype=jnp.int32)
    by_rel = rel_bias.astype(F32)[_rel_bucket(rel)].T * LOG2E
    consts = jnp.stack([by_rel[:, :1], by_rel[:, -1:]], axis=1)
    consts = jnp.broadcast_to(consts[..., None], (N_HEADS, 2, 1, LANES))
    tiles = []
    for d in range(-BIAS_SPAN, BIAS_SPAN + 1):
        pos = lax.slice_in_dim(by_rel, d * tile - lo, (d + 1) * tile - lo, axis=1)
        neg = lax.slice_in_dim(by_rel, (d - 1) * tile - lo, d * tile - lo, axis=1)
        z = jnp.concatenate([pos, neg], axis=1)
        skew = jnp.tile(z, (1, tile))[:, :tile * (2 * tile - 1)].reshape(-1, tile, 2 * tile - 1)
        tiles.append(skew[:, :, :tile])
    return jnp.stack(tiles, axis=1), consts


def _attn(qkv, bias_tiles, bias_consts, lq1, lk1, lq2, lk2, subln_g):
    b, s, _ = qkv.shape
    tile = ATTN_T
    n_k = s // tile
    assert s % tile == 0 and n_k >= ATTN_NEAR and (n_k - ATTN_NEAR) % ATTN_WIDE == 0
    wide = ATTN_WIDE * tile
    n_bias = 2 * BIAS_SPAN + 1
    k_col = ATTN_QK // LANES
    v_col = 2 * ATTN_QK // LANES
    vec = lambda v: v.reshape(1, -1).astype(F32)
    small = lambda n: pl.BlockSpec((1, n), lambda h, bi, i: (0, 0))
    rows = 2 * tile
    return pl.pallas_call(
        functools.partial(_attn_kernel, tile=tile, n_k=n_k),
        grid=(N_HEADS, b, s // tile),
        in_specs=[
            pl.BlockSpec((1, tile, LANES), lambda h, bi, i: (bi, i, h)),
            pl.BlockSpec((1, s, LANES), lambda h, bi, i: (bi, 0, k_col + h)),
            pl.BlockSpec((1, s, LANES), lambda h, bi, i: (bi, 0, v_col + h)),
            pl.BlockSpec((1, n_bias, tile, tile), lambda h, bi, i: (h, 0, 0, 0)),
            pl.BlockSpec((1, 2, 1, LANES), lambda h, bi, i: (h, 0, 0, 0)),
            small(HEAD_DK), small(HEAD_DK), small(HEAD_DK), small(HEAD_DK), small(HEAD_DV),
        ],
        out_specs=pl.BlockSpec((1, tile, LANES), lambda h, bi, i: (bi, i, h)),
        out_shape=jax.ShapeDtypeStruct((b, s, ATTN_V), BF16),
        scratch_shapes=[
            pltpu.VMEM((rows, LANES), BF16),
            pltpu.VMEM((rows, 2 * HEAD_DV), F32),
            pltpu.VMEM((rows, LANES), F32),
            pltpu.VMEM((rows, LANES), F32),
            pltpu.VMEM((2, rows, wide), F32),
            pltpu.VMEM((2, rows, wide), BF16),
            pltpu.VMEM((2, rows, LANES), F32),
        ],
        compiler_params=pltpu.CompilerParams(
            dimension_semantics=("arbitrary", "arbitrary", "arbitrary"),
            vmem_limit_bytes=VMEM_LIMIT_BYTES),
        name="diff_attn",
    )(qkv, qkv, qkv, bias_tiles, bias_consts, vec(lq1), vec(lk1), vec(lq2), vec(lk2), vec(subln_g))


def _outproj_kernel(x_ref, c_ref, a_ref, wc_ref, wa_ref, o_ref):
    o_ref[...] = (x_ref[...]
                  + jnp.dot(c_ref[...], wc_ref[...], preferred_element_type=F32)
                  + jnp.dot(a_ref[...], wa_ref[...], preferred_element_type=F32))


def _outproj(x, conv_out, att, w_out):
    m, d = x.shape
    tm = PROJ_TM
    return pl.pallas_call(
        _outproj_kernel,
        grid=(m // tm,),
        in_specs=[
            pl.BlockSpec((tm, d), lambda i: (i, 0)),
            pl.BlockSpec((tm, CONV_CH), lambda i: (i, 0)),
            pl.BlockSpec((tm, ATTN_V), lambda i: (i, 0)),
            pl.BlockSpec((CONV_CH, d), lambda i: (0, 0)),
            pl.BlockSpec((ATTN_V, d), lambda i: (1, 0)),
        ],
        out_specs=pl.BlockSpec((tm, d), lambda i: (i, 0)),
        out_shape=jax.ShapeDtypeStruct((m, d), F32),
        compiler_params=pltpu.CompilerParams(
            dimension_semantics=("arbitrary",),
            vmem_limit_bytes=VMEM_LIMIT_BYTES),
        name="outproj",
    )(x, conv_out, att, w_out, w_out)


def _trunk(x, w, bias):
    b, s, d = x.shape
    m = b * s
    x0 = x.reshape(m, d)
    x1 = _ffn(x0, w["ffn1_norm"], w["ffn1_w_in"], w["ffn1_w_out"])
    u, qkv = _inproj(x1, w["mix_norm"], w["w_in"])
    conv_out = _conv(u.reshape(b, s, CONV_CH), w["conv_w"], w["conv_b"], w["conv_ln_g"], w["conv_ln_b"])
    att = _attn(qkv.reshape(b, s, -1), *bias, w["lambda_q1"], w["lambda_k1"],
                w["lambda_q2"], w["lambda_k2"], w["subln_g"])
    x2 = _outproj(x1, conv_out.reshape(m, CONV_CH), att.reshape(m, ATTN_V), w["w_out"])
    y = _ffn(x2, w["ffn2_norm"], w["ffn2_w_in"], w["ffn2_w_out"], final_g=w["final_norm"])
    return y.reshape(b, s, d)


def kernel(x_prompt, x_sample, rel_bias, ffn1_norm, ffn1_w_in, ffn1_w_out, mix_norm, w_in, conv_w, conv_b, conv_ln_g, conv_ln_b, lambda_q1, lambda_k1, lambda_q2, lambda_k2, subln_g, w_out, ffn2_norm, ffn2_w_in, ffn2_w_out, final_norm):
    assert ffn1_norm.shape[0] == 1, "single-layer trunk"
    w = dict(
        ffn1_norm=ffn1_norm[0], ffn1_w_in=ffn1_w_in[0].astype(BF16), ffn1_w_out=ffn1_w_out[0].astype(BF16),
        mix_norm=mix_norm[0], w_in=w_in[0].astype(BF16),
        conv_w=conv_w[0], conv_b=conv_b[0], conv_ln_g=conv_ln_g[0], conv_ln_b=conv_ln_b[0],
        lambda_q1=lambda_q1[0], lambda_k1=lambda_k1[0], lambda_q2=lambda_q2[0], lambda_k2=lambda_k2[0],
        subln_g=subln_g[0], w_out=w_out[0].astype(BF16),
        ffn2_norm=ffn2_norm[0], ffn2_w_in=ffn2_w_in[0].astype(BF16), ffn2_w_out=ffn2_w_out[0].astype(BF16),
        final_norm=final_norm,
    )
    bias = _bias_tiles(rel_bias, ATTN_T)
    return (_trunk(x_prompt, w, bias), _trunk(x_sample, w, bias))
```

```python
import functools
import math

import jax
import jax.numpy as jnp
from jax import lax
from jax.experimental import pallas as pl
from jax.experimental.pallas import tpu as pltpu

F32 = jnp.float32
BF16 = jnp.bfloat16

D_MODEL = 2048
CONV_CH = D_MODEL // 2
CONV_WIDTH = 31
CONV_PAD = CONV_WIDTH // 2
N_HEADS = 8
HEAD_DK = 64
HEAD_DV = 2 * HEAD_DK
ATTN_QK = N_HEADS * 2 * HEAD_DK
ATTN_V = N_HEADS * HEAD_DV
D_FF = 5632
N_BUCKETS = 32
MAX_DISTANCE = 128
RMS_EPS = 1e-6
LN_EPS = 1e-5
SUBLN_EPS = 1e-5
LAYER = 0
LAMBDA_INIT = 0.8 - 0.6 * math.exp(-0.3 * LAYER)
LOG2E = math.log2(math.e)
Q_SCALE = HEAD_DK ** -0.5 * LOG2E

LANES = 128
SUBLANES = 8
VMEM_LIMIT_BYTES = 56 * 1024 * 1024

FFN_TM = 1024
FFN_TF = 512
PROJ_TM = 512
CONV_TS = 512
CONV_HALO = 16
CONV_ROWS = 64
CONV_LANES = 256
ATTN_T = 512
BIAS_SPAN = 2
ATTN_WIDE = 2
ATTN_NEAR = 4


def _rms(x, g, eps):
    return x * lax.rsqrt(jnp.mean(x * x, axis=-1, keepdims=True) + eps) * g


def _ffn_kernel(x_ref, g_ref, wg_ref, wu_ref, wo_ref, *rest, n_steps, final):
    if final:
        fg_ref, o_ref, xn_ref = rest
    else:
        o_ref, xn_ref = rest
    j = pl.program_id(1)

    @pl.when(j == 0)
    def _():
        xn_ref[...] = _rms(x_ref[...], g_ref[...], RMS_EPS).astype(BF16)
        o_ref[...] = jnp.zeros_like(o_ref)

    xn = xn_ref[...]
    gate = jnp.dot(xn, wg_ref[...], preferred_element_type=F32)
    up = jnp.dot(xn, wu_ref[...], preferred_element_type=F32)
    act = (gate * jax.nn.sigmoid(gate) * up).astype(BF16)
    o_ref[...] += jnp.dot(act, wo_ref[...], preferred_element_type=F32)

    @pl.when(j == n_steps - 1)
    def _():
        y = x_ref[...] + 0.5 * o_ref[...]
        if final:
            y = _rms(y, fg_ref[...], RMS_EPS)
        o_ref[...] = y


def _ffn(x, norm_g, w_in, w_out, final_g=None):
    m, d = x.shape
    tm, tf = FFN_TM, FFN_TF
    n_steps = D_FF // tf
    final = final_g is not None
    in_specs = [
        pl.BlockSpec((tm, d), lambda i, j: (i, 0), pipeline_mode=pl.Buffered(1)),
        pl.BlockSpec((1, d), lambda i, j: (0, 0)),
        pl.BlockSpec((d, tf), lambda i, j: (0, j)),
        pl.BlockSpec((d, tf), lambda i, j: (0, j + n_steps)),
        pl.BlockSpec((tf, d), lambda i, j: (j, 0)),
    ]
    args = [x, norm_g.reshape(1, d), w_in, w_in, w_out]
    if final:
        in_specs.append(pl.BlockSpec((1, d), lambda i, j: (0, 0)))
        args.append(final_g.reshape(1, d))
    return pl.pallas_call(
        functools.partial(_ffn_kernel, n_steps=n_steps, final=final),
        grid=(m // tm, n_steps),
        in_specs=in_specs,
        out_specs=pl.BlockSpec((tm, d), lambda i, j: (i, 0)),
        out_shape=jax.ShapeDtypeStruct((m, d), F32),
        scratch_shapes=[pltpu.VMEM((tm, d), BF16)],
        compiler_params=pltpu.CompilerParams(
            dimension_semantics=("arbitrary", "arbitrary"),
            vmem_limit_bytes=VMEM_LIMIT_BYTES),
        name="ffn_final" if final else "ffn",
    )(*args)


def _inproj_kernel(x_ref, g_ref, wa_ref, wgate_ref, wqkv_ref, u_ref, qkv_ref):
    xn = _rms(x_ref[...], g_ref[...], RMS_EPS).astype(BF16)
    a = jnp.dot(xn, wa_ref[...], preferred_element_type=F32)
    gate = jnp.dot(xn, wgate_ref[...], preferred_element_type=F32)
    u_ref[...] = a * jax.nn.sigmoid(gate)
    for c, scale in enumerate((Q_SCALE, None, None)):
        cols = slice(c * ATTN_QK, (c + 1) * ATTN_QK)
        h = jnp.dot(xn, wqkv_ref[:, cols], preferred_element_type=F32)
        qkv_ref[:, cols] = (h if scale is None else h * scale).astype(BF16)


def _inproj(x, norm_g, w_in):
    m, d = x.shape
    tm = PROJ_TM
    n_qkv = 2 * ATTN_QK + ATTN_V
    return pl.pallas_call(
        _inproj_kernel,
        grid=(m // tm,),
        in_specs=[
            pl.BlockSpec((tm, d), lambda i: (i, 0)),
            pl.BlockSpec((1, d), lambda i: (0, 0)),
            pl.BlockSpec((d, CONV_CH), lambda i: (0, 0)),
            pl.BlockSpec((d, CONV_CH), lambda i: (0, 1)),
            pl.BlockSpec((d, n_qkv), lambda i: (0, 0)),
        ],
        out_specs=[
            pl.BlockSpec((tm, CONV_CH), lambda i: (i, 0)),
            pl.BlockSpec((tm, n_qkv), lambda i: (i, 0)),
        ],
        out_shape=[
            jax.ShapeDtypeStruct((m, CONV_CH), F32),
            jax.ShapeDtypeStruct((m, n_qkv), BF16),
        ],
        compiler_params=pltpu.CompilerParams(
            dimension_semantics=("arbitrary",),
            vmem_limit_bytes=VMEM_LIMIT_BYTES),
        name="inproj",
    )(x, norm_g.reshape(1, d), w_in, w_in, w_in[:, 2 * CONV_CH:])


def _conv_kernel(prev_ref, cur_ref, next_ref, w_ref, cb_ref, lg_ref, lb_ref, o_ref, buf_ref,
                 *, ts, n_blocks):
    i = pl.program_id(1)
    halo = CONV_HALO
    prev = prev_ref[0]
    nxt = next_ref[0]
    buf_ref[0:halo, :] = jnp.where(i == 0, jnp.zeros_like(prev), prev)
    buf_ref[halo:halo + ts, :] = cur_ref[0]
    buf_ref[halo + ts:, :] = jnp.where(i == n_blocks - 1, jnp.zeros_like(nxt), nxt)

    rows = CONV_ROWS
    first = halo - CONV_PAD
    for c in range(ts // rows):
        r0 = c * rows
        parts = []
        for g in range(CONV_CH // CONV_LANES):
            lanes = slice(g * CONV_LANES, (g + 1) * CONV_LANES)
            y = None
            for b in range(SUBLANES):
                zb = None
                for a in range((CONV_WIDTH - b + SUBLANES - 1) // SUBLANES):
                    t = SUBLANES * a + b
                    lo = r0 + SUBLANES * a
                    w_tap = jnp.concatenate([w_ref[t, :, lanes]] * (rows // SUBLANES + 1), axis=0)
                    term = buf_ref[lo:lo + rows + SUBLANES, lanes] * w_tap
                    zb = term if zb is None else zb + term
                k = first + b
                if k % SUBLANES:
                    zb = pltpu.roll(zb, rows + SUBLANES - k, 0)
                    k = 0
                shifted = zb[k:k + rows, :]
                y = shifted if y is None else y + shifted
            parts.append(y)
        y = jnp.concatenate(parts, axis=1) + cb_ref[...]
        mu = jnp.mean(y, axis=-1, keepdims=True)
        yc = y - mu
        var = jnp.mean(yc * yc, axis=-1, keepdims=True)
        z = yc * lax.rsqrt(var + LN_EPS) * lg_ref[...] + lb_ref[...]
        o_ref[0, r0:r0 + rows, :] = (z * jax.nn.sigmoid(z)).astype(BF16)


def _conv(u, conv_w, conv_b, ln_g, ln_b):
    b, s, c = u.shape
    ts = CONV_TS
    n_blocks = s // ts
    per = ts // CONV_HALO
    n_halo = s // CONV_HALO
    row = lambda v: v.reshape(1, c)
    return pl.pallas_call(
        functools.partial(_conv_kernel, ts=ts, n_blocks=n_blocks),
        grid=(b, n_blocks),
        in_specs=[
            pl.BlockSpec((1, CONV_HALO, c), lambda bi, i: (bi, jnp.maximum(i * per - 1, 0), 0)),
            pl.BlockSpec((1, ts, c), lambda bi, i: (bi, i, 0)),
            pl.BlockSpec((1, CONV_HALO, c), lambda bi, i: (bi, jnp.minimum((i + 1) * per, n_halo - 1), 0)),
            pl.BlockSpec((CONV_WIDTH, SUBLANES, c), lambda bi, i: (0, 0, 0)),
            pl.BlockSpec((1, c), lambda bi, i: (0, 0)),
            pl.BlockSpec((1, c), lambda bi, i: (0, 0)),
            pl.BlockSpec((1, c), lambda bi, i: (0, 0)),
        ],
        out_specs=pl.BlockSpec((1, ts, c), lambda bi, i: (bi, i, 0)),
        out_shape=jax.ShapeDtypeStruct((b, s, c), BF16),
        scratch_shapes=[pltpu.VMEM((ts + 2 * CONV_HALO, c), F32)],
        compiler_params=pltpu.CompilerParams(
            dimension_semantics=("arbitrary", "arbitrary"),
            vmem_limit_bytes=VMEM_LIMIT_BYTES),
        name="conv",
    )(u, u, u, jnp.broadcast_to(conv_w[:, None, :], (CONV_WIDTH, SUBLANES, c)),
      row(conv_b), row(ln_g), row(ln_b))


def _attn_kernel(q_ref, k_ref, v_ref, bias_ref, c_ref, lq1_ref, lk1_ref, lq2_ref, lk2_ref, g_ref, o_ref,
                 q2_ref, acc_ref, m_ref, alpha_ref, s_ref, p_ref, mc_ref, *, tile, n_k):
    i = pl.program_id(2)
    q = q_ref[0]
    lane = lax.broadcasted_iota(jnp.int32, q.shape, 1)
    zero = jnp.zeros_like(q)
    q2_ref[0:tile, :] = jnp.where(lane < HEAD_DK, q, zero)
    q2_ref[tile:, :] = jnp.where(lane >= HEAD_DK, q, zero)
    acc_ref[...] = jnp.zeros_like(acc_ref)
    m_ref[...] = jnp.full_like(m_ref, -jnp.inf)

    wide = ATTN_WIDE * tile
    n_wide = (n_k - ATTN_NEAR) // ATTN_WIDE
    n_steps = n_wide + ATTN_NEAR
    w0 = jnp.clip((i + 1) // ATTN_WIDE * ATTN_WIDE - ATTN_WIDE, 0, n_k - ATTN_NEAR)
    far_before = w0 // ATTN_WIDE

    def keys_of(t, near):
        if near:
            return pl.multiple_of((w0 + (t - n_wide)) * tile, tile), tile
        first = jnp.where(t < far_before, t, t + ATTN_NEAR // ATTN_WIDE)
        return pl.multiple_of(first * wide, wide), wide

    def scores(t, slot, near):
        start, width = keys_of(t, near)
        k = k_ref[0, pl.ds(start, width), :]
        s = lax.dot_general(q2_ref[...], k, (((1,), (1,)), ((), ())), preferred_element_type=F32)
        if near:
            d = jnp.clip(w0 + (t - n_wide) - i, -BIAS_SPAN, BIAS_SPAN)
            bias = bias_ref[0, d + BIAS_SPAN]
            s = s + jnp.concatenate([bias, bias], axis=0)
        s_ref[slot, :, 0:width] = s
        mc_ref[slot] = jnp.broadcast_to(jnp.max(s, axis=1, keepdims=True), mc_ref.shape[1:])

    def softmax(t, slot, near):
        width = tile if near else wide
        m_prev = m_ref[...]
        if near:
            m_new = jnp.maximum(m_prev, mc_ref[slot])
            shift = m_new
        else:
            const = jnp.where(t < far_before, c_ref[0, 0], c_ref[0, 1])
            m_new = jnp.maximum(m_prev, mc_ref[slot] + const)
            shift = m_new - const
        p = jnp.exp2(s_ref[slot, :, 0:width] - jnp.concatenate([shift] * (width // LANES), axis=1))
        alpha_ref[...] = jnp.exp2(m_prev - m_new)
        m_ref[...] = m_new
        p_ref[slot, :, 0:width] = p.astype(BF16)

    def values(t, slot, near):
        start, width = keys_of(t, near)
        v = v_ref[0, pl.ds(start, width), :]
        v_ones = jnp.concatenate([v, jnp.ones_like(v)], axis=1)
        alpha = jnp.concatenate([alpha_ref[...]] * 2, axis=1)
        acc_ref[...] = alpha * acc_ref[...] + jnp.dot(p_ref[slot, :, 0:width], v_ones,
                                                      preferred_element_type=F32)

    is_near = lambda t: t >= n_wide

    def far_pair(u, carry):
        for parity in range(2):
            t = 2 * u + 1 + parity
            slot = 1 - parity
            values(t - 1, 1 - slot, near=False)
            scores(t + 1, 1 - slot, near=False)
            softmax(t, slot, near=False)
        return carry

    n_pairs = max(n_wide - 2, 0) // 2
    scores(0, 0, is_near(0))
    scores(1, 1, is_near(1))
    softmax(0, 0, is_near(0))
    lax.fori_loop(0, n_pairs, far_pair, 0)
    for t in range(2 * n_pairs + 1, n_steps):
        slot = t % 2
        values(t - 1, 1 - slot, is_near(t - 1))
        if t + 1 < n_steps:
            scores(t + 1, 1 - slot, is_near(t + 1))
        softmax(t, slot, is_near(t))
    values(n_steps - 1, (n_steps - 1) % 2, is_near(n_steps - 1))

    lam = (jnp.exp(jnp.sum(lq1_ref[...] * lk1_ref[...], axis=-1, keepdims=True))
           - jnp.exp(jnp.sum(lq2_ref[...] * lk2_ref[...], axis=-1, keepdims=True))
           + LAMBDA_INIT)
    o1 = acc_ref[0:tile, 0:HEAD_DV] / acc_ref[0:tile, HEAD_DV:]
    o2 = acc_ref[tile:, 0:HEAD_DV] / acc_ref[tile:, HEAD_DV:]
    o = o1 - lam * o2
    o = _rms(o, g_ref[...], SUBLN_EPS) * (1.0 - LAMBDA_INIT)
    o_ref[0] = o.astype(BF16)


def _rel_bucket(rel):
    half = N_BUCKETS // 2
    max_exact = half // 2
    ret = (rel > 0).astype(jnp.int32) * half
    n = jnp.abs(rel)
    nf = jnp.maximum(n, 1).astype(jnp.float32)
    large = max_exact + (jnp.log(nf / max_exact) / math.log(MAX_DISTANCE / max_exact)
                         * (half - max_exact)).astype(jnp.int32)
    large = jnp.minimum(large, half - 1)
    return ret + jnp.where(n < max_exact, n, large)


def _bias_tiles(rel_bias, tile):
    assert tile >= MAX_DISTANCE
    lo = -(BIAS_SPAN + 1) * tile
    rel = jnp.arange(lo, -lo, dtype=jnp.int32)
    by_rel = rel_bias.astype(F32)[_rel_bucket(rel)].T * LOG2E
    consts = jnp.stack([by_rel[:, :1], by_rel[:, -1:]], axis=1)
    consts = jnp.broadcast_to(consts[..., None], (N_HEADS, 2, 1, LANES))
    tiles = []
    for d in range(-BIAS_SPAN, BIAS_SPAN + 1):
        pos = lax.slice_in_dim(by_rel, d * tile - lo, (d + 1) * tile - lo, axis=1)
        neg = lax.slice_in_dim(by_rel, (d - 1) * tile - lo, d * tile - lo, axis=1)
        z = jnp.concatenate([pos, neg], axis=1)
        skew = jnp.tile(z, (1, tile))[:, :tile * (2 * tile - 1)].reshape(-1, tile, 2 * tile - 1)
        tiles.append(skew[:, :, :tile])
    return jnp.stack(tiles, axis=1), consts


def _attn(qkv, bias_tiles, bias_consts, lq1, lk1, lq2, lk2, subln_g):
    b, s, _ = qkv.shape
    tile = ATTN_T
    n_k = s // tile
    assert s % tile == 0 and n_k >= ATTN_NEAR and (n_k - ATTN_NEAR) % ATTN_WIDE == 0
    wide = ATTN_WIDE * tile
    n_bias = 2 * BIAS_SPAN + 1
    k_col = ATTN_QK // LANES
    v_col = 2 * ATTN_QK // LANES
    vec = lambda v: v.reshape(1, -1).astype(F32)
    small = lambda n: pl.BlockSpec((1, n), lambda h, bi, i: (0, 0))
    rows = 2 * tile
    return pl.pallas_call(
        functools.partial(_attn_kernel, tile=tile, n_k=n_k),
        grid=(N_HEADS, b, s // tile),
        in_specs=[
            pl.BlockSpec((1, tile, LANES), lambda h, bi, i: (bi, i, h)),
            pl.BlockSpec((1, s, LANES), lambda h, bi, i: (bi, 0, k_col + h)),
            pl.BlockSpec((1, s, LANES), lambda h, bi, i: (bi, 0, v_col + h)),
            pl.BlockSpec((1, n_bias, tile, tile), lambda h, bi, i: (h, 0, 0, 0)),
            pl.BlockSpec((1, 2, 1, LANES), lambda h, bi, i: (h, 0, 0, 0)),
            small(HEAD_DK), small(HEAD_DK), small(HEAD_DK), small(HEAD_DK), small(HEAD_DV),
        ],
        out_specs=pl.BlockSpec((1, tile, LANES), lambda h, bi, i: (bi, i, h)),
        out_shape=jax.ShapeDtypeStruct((b, s, ATTN_V), BF16),
        scratch_shapes=[
            pltpu.VMEM((rows, LANES), BF16),
            pltpu.VMEM((rows, 2 * HEAD_DV), F32),
            pltpu.VMEM((rows, LANES), F32),
            pltpu.VMEM((rows, LANES), F32),
            pltpu.VMEM((2, rows, wide), F32),
            pltpu.VMEM((2, rows, wide), BF16),
            pltpu.VMEM((2, rows, LANES), F32),
        ],
        compiler_params=pltpu.CompilerParams(
            dimension_semantics=("arbitrary", "arbitrary", "arbitrary"),
            vmem_limit_bytes=VMEM_LIMIT_BYTES),
        name="diff_attn",
    )(qkv, qkv, qkv, bias_tiles, bias_consts, vec(lq1), vec(lk1), vec(lq2), vec(lk2), vec(subln_g))


def _outproj_kernel(x_ref, c_ref, a_ref, wc_ref, wa_ref, o_ref):
    o_ref[...] = (x_ref[...]
                  + jnp.dot(c_ref[...], wc_ref[...], preferred_element_type=F32)
                  + jnp.dot(a_ref[...], wa_ref[...], preferred_element_type=F32))


def _outproj(x, conv_out, att, w_out):
    m, d = x.shape
    tm = PROJ_TM
    return pl.pallas_call(
        _outproj_kernel,
        grid=(m // tm,),
        in_specs=[
            pl.BlockSpec((tm, d), lambda i: (i, 0)),
            pl.BlockSpec((tm, CONV_CH), lambda i: (i, 0)),
            pl.BlockSpec((tm, ATTN_V), lambda i: (i, 0)),
            pl.BlockSpec((CONV_CH, d), lambda i: (0, 0)),
            pl.BlockSpec((ATTN_V, d), lambda i: (1, 0)),
        ],
        out_specs=pl.BlockSpec((tm, d), lambda i: (i, 0)),
        out_shape=jax.ShapeDtypeStruct((m, d), F32),
        compiler_params=pltpu.CompilerParams(
            dimension_semantics=("arbitrary",),
            vmem_limit_bytes=VMEM_LIMIT_BYTES),
        name="outproj",
    )(x, conv_out, att, w_out, w_out)


def _trunk(x, w, bias):
    b, s, d = x.shape
    m = b * s
    x0 = x.reshape(m, d)
    x1 = _ffn(x0, w["ffn1_norm"], w["ffn1_w_in"], w["ffn1_w_out"])
    u, qkv = _inproj(x1, w["mix_norm"], w["w_in"])
    conv_out = _conv(u.reshape(b, s, CONV_CH), w["conv_w"], w["conv_b"], w["conv_ln_g"], w["conv_ln_b"])
    att = _attn(qkv.reshape(b, s, -1), *bias, w["lambda_q1"], w["lambda_k1"],
                w["lambda_q2"], w["lambda_k2"], w["subln_g"])
    x2 = _outproj(x1, conv_out.reshape(m, CONV_CH), att.reshape(m, ATTN_V), w["w_out"])
    y = _ffn(x2, w["ffn2_norm"], w["ffn2_w_in"], w["ffn2_w_out"], final_g=w["final_norm"])
    return y.reshape(b, s, d)


def kernel(x_prompt, x_sample, rel_bias, ffn1_norm, ffn1_w_in, ffn1_w_out, mix_norm, w_in, conv_w, conv_b, conv_ln_g, conv_ln_b, lambda_q1, lambda_k1, lambda_q2, lambda_k2, subln_g, w_out, ffn2_norm, ffn2_w_in, ffn2_w_out, final_norm):
    assert ffn1_norm.shape[0] == 1, "single-layer trunk"
    w = dict(
        ffn1_norm=ffn1_norm[0], ffn1_w_in=ffn1_w_in[0].astype(BF16), ffn1_w_out=ffn1_w_out[0].astype(BF16),
        mix_norm=mix_norm[0], w_in=w_in[0].astype(BF16),
        conv_w=conv_w[0], conv_b=conv_b[0], conv_ln_g=conv_ln_g[0], conv_ln_b=conv_ln_b[0],
        lambda_q1=lambda_q1[0], lambda_k1=lambda_k1[0], lambda_q2=lambda_q2[0], lambda_k2=lambda_k2[0],
        subln_g=subln_g[0], w_out=w_out[0].astype(BF16),
        ffn2_norm=ffn2_norm[0], ffn2_w_in=ffn2_w_in[0].astype(BF16), ffn2_w_out=ffn2_w_out[0].astype(BF16),
        final_norm=final_norm,
    )
    bias = _bias_tiles(rel_bias, ATTN_T)
    return (_trunk(x_prompt, w, bias), _trunk(x_sample, w, bias))
```

```python
import functools
import math

import jax
import jax.numpy as jnp
from jax import lax
from jax.experimental import pallas as pl
from jax.experimental.pallas import tpu as pltpu

F32 = jnp.float32
BF16 = jnp.bfloat16

D_MODEL = 2048
CONV_CH = D_MODEL // 2
CONV_WIDTH = 31
CONV_PAD = CONV_WIDTH // 2
N_HEADS = 8
HEAD_DK = 64
HEAD_DV = 2 * HEAD_DK
ATTN_QK = N_HEADS * 2 * HEAD_DK
ATTN_V = N_HEADS * HEAD_DV
D_FF = 5632
N_BUCKETS = 32
MAX_DISTANCE = 128
RMS_EPS = 1e-6
LN_EPS = 1e-5
SUBLN_EPS = 1e-5
LAYER = 0
LAMBDA_INIT = 0.8 - 0.6 * math.exp(-0.3 * LAYER)
LOG2E = math.log2(math.e)
Q_SCALE = HEAD_DK ** -0.5 * LOG2E

LANES = 128
SUBLANES = 8
VMEM_LIMIT_BYTES = 56 * 1024 * 1024

FFN_TM = 512
FFN_TF = 512
PROJ_TM = 512
CONV_TS = 512
CONV_HALO = 16
CONV_ROWS = 64
CONV_LANES = 256
ATTN_T = 512
BIAS_SPAN = 2
ATTN_WIDE = 2
ATTN_NEAR = 4


def _rms(x, g, eps):
    return x * lax.rsqrt(jnp.mean(x * x, axis=-1, keepdims=True) + eps) * g


def _ffn_kernel(x_ref, g_ref, wg_ref, wu_ref, wo_ref, *rest, n_steps, final, prenormed):
    rest = list(rest)
    fg_ref = rest.pop(0) if final else None
    o_ref = rest.pop(0)
    xn_ref = g_ref if prenormed else rest.pop(0)
    acc_ref = rest.pop(0)
    j = pl.program_id(1)

    @pl.when(j == 0)
    def _():
        if not prenormed:
            xn_ref[...] = _rms(x_ref[...], g_ref[...], RMS_EPS).astype(BF16)
        acc_ref[...] = jnp.zeros_like(acc_ref)

    xn = xn_ref[...]
    gate = jnp.dot(xn, wg_ref[...], preferred_element_type=F32)
    up = jnp.dot(xn, wu_ref[...], preferred_element_type=F32)
    act = (gate * jax.nn.sigmoid(gate) * up).astype(BF16)
    acc_ref[...] += jnp.dot(act, wo_ref[...], preferred_element_type=F32)

    @pl.when(j == n_steps - 1)
    def _():
        y = x_ref[...] + 0.5 * acc_ref[...]
        if final:
            y = _rms(y, fg_ref[...], RMS_EPS)
        o_ref[...] = y


def _ffn(x, norm_g, w_in, w_out, final_g=None, xn=None):
    m, d = x.shape
    tm, tf = FFN_TM, FFN_TF
    n_steps = D_FF // tf
    final = final_g is not None
    prenormed = xn is not None
    in_specs = [
        pl.BlockSpec((tm, d), lambda i, j: (i, 0)),
        pl.BlockSpec((tm, d), lambda i, j: (i, 0)) if prenormed else pl.BlockSpec((1, d), lambda i, j: (0, 0)),
        pl.BlockSpec((d, tf), lambda i, j: (0, j)),
        pl.BlockSpec((d, tf), lambda i, j: (0, j + n_steps)),
        pl.BlockSpec((tf, d), lambda i, j: (j, 0)),
    ]
    args = [x, xn if prenormed else norm_g.reshape(1, d), w_in, w_in, w_out]
    if final:
        in_specs.append(pl.BlockSpec((1, d), lambda i, j: (0, 0)))
        args.append(final_g.reshape(1, d))
    return pl.pallas_call(
        functools.partial(_ffn_kernel, n_steps=n_steps, final=final, prenormed=prenormed),
        grid=(m // tm, n_steps),
        in_specs=in_specs,
        out_specs=pl.BlockSpec((tm, d), lambda i, j: (i, 0)),
        out_shape=jax.ShapeDtypeStruct((m, d), F32),
        scratch_shapes=([] if prenormed else [pltpu.VMEM((tm, d), BF16)]) + [pltpu.VMEM((tm, d), F32)],
        compiler_params=pltpu.CompilerParams(
            dimension_semantics=("arbitrary", "arbitrary"),
            vmem_limit_bytes=VMEM_LIMIT_BYTES),
        name="ffn_final" if final else "ffn",
    )(*args)


def _inproj_kernel(x_ref, g_ref, wa_ref, wgate_ref, wqkv_ref, u_ref, qkv_ref):
    xn = _rms(x_ref[...], g_ref[...], RMS_EPS).astype(BF16)
    a = jnp.dot(xn, wa_ref[...], preferred_element_type=F32)
    gate = jnp.dot(xn, wgate_ref[...], preferred_element_type=F32)
    u_ref[...] = a * jax.nn.sigmoid(gate)
    for c, scale in enumerate((Q_SCALE, None, None)):
        cols = slice(c * ATTN_QK, (c + 1) * ATTN_QK)
        h = jnp.dot(xn, wqkv_ref[:, cols], preferred_element_type=F32)
        qkv_ref[:, cols] = (h if scale is None else h * scale).astype(BF16)


def _inproj(x, norm_g, w_in):
    m, d = x.shape
    tm = PROJ_TM
    n_qkv = 2 * ATTN_QK + ATTN_V
    return pl.pallas_call(
        _inproj_kernel,
        grid=(m // tm,),
        in_specs=[
            pl.BlockSpec((tm, d), lambda i: (i, 0)),
            pl.BlockSpec((1, d), lambda i: (0, 0)),
            pl.BlockSpec((d, CONV_CH), lambda i: (0, 0)),
            pl.BlockSpec((d, CONV_CH), lambda i: (0, 1)),
            pl.BlockSpec((d, n_qkv), lambda i: (0, 0)),
        ],
        out_specs=[
            pl.BlockSpec((tm, CONV_CH), lambda i: (i, 0)),
            pl.BlockSpec((tm, n_qkv), lambda i: (i, 0)),
        ],
        out_shape=[
            jax.ShapeDtypeStruct((m, CONV_CH), F32),
            jax.ShapeDtypeStruct((m, n_qkv), BF16),
        ],
        compiler_params=pltpu.CompilerParams(
            dimension_semantics=("arbitrary",),
            vmem_limit_bytes=VMEM_LIMIT_BYTES),
        name="inproj",
    )(x, norm_g.reshape(1, d), w_in, w_in, w_in[:, 2 * CONV_CH:])


def _conv_kernel(prev_ref, cur_ref, next_ref, w_ref, cb_ref, lg_ref, lb_ref, o_ref, buf_ref,
                 *, ts, n_blocks):
    i = pl.program_id(1)
    halo = CONV_HALO
    prev = prev_ref[0]
    nxt = next_ref[0]
    buf_ref[0:halo, :] = jnp.where(i == 0, jnp.zeros_like(prev), prev)
    buf_ref[halo:halo + ts, :] = cur_ref[0]
    buf_ref[halo + ts:, :] = jnp.where(i == n_blocks - 1, jnp.zeros_like(nxt), nxt)

    rows = CONV_ROWS
    first = halo - CONV_PAD
    for c in range(ts // rows):
        r0 = c * rows
        parts = []
        for g in range(CONV_CH // CONV_LANES):
            lanes = slice(g * CONV_LANES, (g + 1) * CONV_LANES)
            y = None
            for b in range(SUBLANES):
                zb = None
                for a in range((CONV_WIDTH - b + SUBLANES - 1) // SUBLANES):
                    t = SUBLANES * a + b
                    lo = r0 + SUBLANES * a
                    w_tap = jnp.concatenate([w_ref[t, :, lanes]] * (rows // SUBLANES + 1), axis=0)
                    term = buf_ref[lo:lo + rows + SUBLANES, lanes] * w_tap
                    zb = term if zb is None else zb + term
                k = first + b
                if k % SUBLANES:
                    zb = pltpu.roll(zb, rows + SUBLANES - k, 0)
                    k = 0
                shifted = zb[k:k + rows, :]
                y = shifted if y is None else y + shifted
            parts.append(y)
        y = jnp.concatenate(parts, axis=1) + cb_ref[...]
        mu = jnp.mean(y, axis=-1, keepdims=True)
        yc = y - mu
        var = jnp.mean(yc * yc, axis=-1, keepdims=True)
        z = yc * lax.rsqrt(var + LN_EPS) * lg_ref[...] + lb_ref[...]
        o_ref[0, r0:r0 + rows, :] = (z * jax.nn.sigmoid(z)).astype(BF16)


def _conv(u, conv_w, conv_b, ln_g, ln_b):
    b, s, c = u.shape
    ts = CONV_TS
    n_blocks = s // ts
    per = ts // CONV_HALO
    n_halo = s // CONV_HALO
    row = lambda v: v.reshape(1, c)
    return pl.pallas_call(
        functools.partial(_conv_kernel, ts=ts, n_blocks=n_blocks),
        grid=(b, n_blocks),
        in_specs=[
            pl.BlockSpec((1, CONV_HALO, c), lambda bi, i: (bi, jnp.maximum(i * per - 1, 0), 0)),
            pl.BlockSpec((1, ts, c), lambda bi, i: (bi, i, 0)),
            pl.BlockSpec((1, CONV_HALO, c), lambda bi, i: (bi, jnp.minimum((i + 1) * per, n_halo - 1), 0)),
            pl.BlockSpec((CONV_WIDTH, SUBLANES, c), lambda bi, i: (0, 0, 0)),
            pl.BlockSpec((1, c), lambda bi, i: (0, 0)),
            pl.BlockSpec((1, c), lambda bi, i: (0, 0)),
            pl.BlockSpec((1, c), lambda bi, i: (0, 0)),
        ],
        out_specs=pl.BlockSpec((1, ts, c), lambda bi, i: (bi, i, 0)),
        out_shape=jax.ShapeDtypeStruct((b, s, c), BF16),
        scratch_shapes=[pltpu.VMEM((ts + 2 * CONV_HALO, c), F32)],
        compiler_params=pltpu.CompilerParams(
            dimension_semantics=("arbitrary", "arbitrary"),
            vmem_limit_bytes=VMEM_LIMIT_BYTES),
        name="conv",
    )(u, u, u, jnp.broadcast_to(conv_w[:, None, :], (CONV_WIDTH, SUBLANES, c)),
      row(conv_b), row(ln_g), row(ln_b))


def _attn_kernel(q_ref, k_ref, v_ref, bias_ref, c_ref, lq1_ref, lk1_ref, lq2_ref, lk2_ref, g_ref, o_ref,
                 q2_ref, acc_ref, m_ref, alpha_ref, s_ref, p_ref, mc_ref, *, tile, n_k):
    i = pl.program_id(2)
    q = q_ref[0]
    lane = lax.broadcasted_iota(jnp.int32, q.shape, 1)
    zero = jnp.zeros_like(q)
    q2_ref[0:tile, :] = jnp.where(lane < HEAD_DK, q, zero)
    q2_ref[tile:, :] = jnp.where(lane >= HEAD_DK, q, zero)
    acc_ref[...] = jnp.zeros_like(acc_ref)
    m_ref[...] = jnp.full_like(m_ref, -jnp.inf)

    wide = ATTN_WIDE * tile
    n_wide = (n_k - ATTN_NEAR) // ATTN_WIDE
    n_steps = n_wide + ATTN_NEAR
    w0 = jnp.clip((i + 1) // ATTN_WIDE * ATTN_WIDE - ATTN_WIDE, 0, n_k - ATTN_NEAR)
    far_before = w0 // ATTN_WIDE

    def keys_of(t, near):
        if near:
            return pl.multiple_of((w0 + (t - n_wide)) * tile, tile), tile
        first = jnp.where(t < far_before, t, t + ATTN_NEAR // ATTN_WIDE)
        return pl.multiple_of(first * wide, wide), wide

    def scores(t, slot, near):
        start, width = keys_of(t, near)
        k = k_ref[0, pl.ds(start, width), :]
        s = lax.dot_general(q2_ref[...], k, (((1,), (1,)), ((), ())), preferred_element_type=F32)
        if near:
            d = jnp.clip(w0 + (t - n_wide) - i, -BIAS_SPAN, BIAS_SPAN)
            bias = bias_ref[0, d + BIAS_SPAN]
            s = s + jnp.concatenate([bias, bias], axis=0)
        s_ref[slot, :, 0:width] = s
        mc_ref[slot] = jnp.broadcast_to(jnp.max(s, axis=1, keepdims=True), mc_ref.shape[1:])

    def softmax(t, slot, near):
        width = tile if near else wide
        m_prev = m_ref[...]
        if near:
            m_new = jnp.maximum(m_prev, mc_ref[slot])
            shift = m_new
        else:
            const = jnp.where(t < far_before, c_ref[0, 0], c_ref[0, 1])
            m_new = jnp.maximum(m_prev, mc_ref[slot] + const)
            shift = m_new - const
        p = jnp.exp2(s_ref[slot, :, 0:width] - jnp.concatenate([shift] * (width // LANES), axis=1))
        alpha_ref[...] = jnp.exp2(m_prev - m_new)
        m_ref[...] = m_new
        p_ref[slot, :, 0:width] = p.astype(BF16)

    def values(t, slot, near):
        start, width = keys_of(t, near)
        v = v_ref[0, pl.ds(start, width), :]
        v_ones = jnp.concatenate([v, jnp.ones_like(v)], axis=1)
        alpha = jnp.concatenate([alpha_ref[...]] * 2, axis=1)
        acc_ref[...] = alpha * acc_ref[...] + jnp.dot(p_ref[slot, :, 0:width], v_ones,
                                                      preferred_element_type=F32)

    is_near = lambda t: t >= n_wide

    def far_pair(u, carry):
        for parity in range(2):
            t = 2 * u + 1 + parity
            slot = 1 - parity
            values(t - 1, 1 - slot, near=False)
            scores(t + 1, 1 - slot, near=False)
            softmax(t, slot, near=False)
        return carry

    n_pairs = max(n_wide - 2, 0) // 2
    scores(0, 0, is_near(0))
    scores(1, 1, is_near(1))
    softmax(0, 0, is_near(0))
    lax.fori_loop(0, n_pairs, far_pair, 0)
    for t in range(2 * n_pairs + 1, n_steps):
        slot = t % 2
        values(t - 1, 1 - slot, is_near(t - 1))
        if t + 1 < n_steps:
            scores(t + 1, 1 - slot, is_near(t + 1))
        softmax(t, slot, is_near(t))
    values(n_steps - 1, (n_steps - 1) % 2, is_near(n_steps - 1))

    lam = (jnp.exp(jnp.sum(lq1_ref[...] * lk1_ref[...], axis=-1, keepdims=True))
           - jnp.exp(jnp.sum(lq2_ref[...] * lk2_ref[...], axis=-1, keepdims=True))
           + LAMBDA_INIT)
    o1 = acc_ref[0:tile, 0:HEAD_DV] / acc_ref[0:tile, HEAD_DV:]
    o2 = acc_ref[tile:, 0:HEAD_DV] / acc_ref[tile:, HEAD_DV:]
    o = o1 - lam * o2
    o = _rms(o, g_ref[...], SUBLN_EPS) * (1.0 - LAMBDA_INIT)
    o_ref[0] = o.astype(BF16)


def _rel_bucket(rel):
    half = N_BUCKETS // 2
    max_exact = half // 2
    ret = (rel > 0).astype(jnp.int32) * half
    n = jnp.abs(rel)
    nf = jnp.maximum(n, 1).astype(jnp.float32)
    large = max_exact + (jnp.log(nf / max_exact) / math.log(MAX_DISTANCE / max_exact)
                         * (half - max_exact)).astype(jnp.int32)
    large = jnp.minimum(large, half - 1)
    return ret + jnp.where(n < max_exact, n, large)


def _bias_tiles(rel_bias, tile):
    assert tile >= MAX_DISTANCE
    lo = -(BIAS_SPAN + 1) * tile
    rel = jnp.arange(lo, -lo, dtype=jnp.int32)
    by_rel = rel_bias.astype(F32)[_rel_bucket(rel)].T * LOG2E
    consts = jnp.stack([by_rel[:, :1], by_rel[:, -1:]], axis=1)
    consts = jnp.broadcast_to(consts[..., None], (N_HEADS, 2, 1, LANES))
    tiles = []
    for d in range(-BIAS_SPAN, BIAS_SPAN + 1):
        pos = lax.slice_in_dim(by_rel, d * tile - lo, (d + 1) * tile - lo, axis=1)
        neg = lax.slice_in_dim(by_rel, (d - 1) * tile - lo, d * tile - lo, axis=1)
        z = jnp.concatenate([pos, neg], axis=1)
        skew = jnp.tile(z, (1, tile))[:, :tile * (2 * tile - 1)].reshape(-1, tile, 2 * tile - 1)
        tiles.append(skew[:, :, :tile])
    return jnp.stack(tiles, axis=1), consts


def _attn(qkv, bias_tiles, bias_consts, lq1, lk1, lq2, lk2, subln_g):
    b, s, _ = qkv.shape
    tile = ATTN_T
    n_k = s // tile
    assert s % tile == 0 and n_k >= ATTN_NEAR and (n_k - ATTN_NEAR) % ATTN_WIDE == 0
    wide = ATTN_WIDE * tile
    n_bias = 2 * BIAS_SPAN + 1
    k_col = ATTN_QK // LANES
    v_col = 2 * ATTN_QK // LANES
    vec = lambda v: v.reshape(1, -1).astype(F32)
    small = lambda n: pl.BlockSpec((1, n), lambda h, bi, i: (0, 0))
    rows = 2 * tile
    return pl.pallas_call(
        functools.partial(_attn_kernel, tile=tile, n_k=n_k),
        grid=(N_HEADS, b, s // tile),
        in_specs=[
            pl.BlockSpec((1, tile, LANES), lambda h, bi, i: (bi, i, h)),
            pl.BlockSpec((1, s, LANES), lambda h, bi, i: (bi, 0, k_col + h)),
            pl.BlockSpec((1, s, LANES), lambda h, bi, i: (bi, 0, v_col + h)),
            pl.BlockSpec((1, n_bias, tile, tile), lambda h, bi, i: (h, 0, 0, 0)),
            pl.BlockSpec((1, 2, 1, LANES), lambda h, bi, i: (h, 0, 0, 0)),
            small(HEAD_DK), small(HEAD_DK), small(HEAD_DK), small(HEAD_DK), small(HEAD_DV),
        ],
        out_specs=pl.BlockSpec((1, tile, LANES), lambda h, bi, i: (bi, i, h)),
        out_shape=jax.ShapeDtypeStruct((b, s, ATTN_V), BF16),
        scratch_shapes=[
            pltpu.VMEM((rows, LANES), BF16),
            pltpu.VMEM((rows, 2 * HEAD_DV), F32),
            pltpu.VMEM((rows, LANES), F32),
            pltpu.VMEM((rows, LANES), F32),
            pltpu.VMEM((2, rows, wide), F32),
            pltpu.VMEM((2, rows, wide), BF16),
            pltpu.VMEM((2, rows, LANES), F32),
        ],
        compiler_params=pltpu.CompilerParams(
            dimension_semantics=("arbitrary", "arbitrary", "arbitrary"),
            vmem_limit_bytes=VMEM_LIMIT_BYTES),
        name="diff_attn",
    )(qkv, qkv, qkv, bias_tiles, bias_consts, vec(lq1), vec(lk1), vec(lq2), vec(lk2), vec(subln_g))


def _outproj_kernel(x_ref, c_ref, a_ref, wc_ref, wa_ref, g_ref, o_ref, on_ref):
    y = (x_ref[...]
         + jnp.dot(c_ref[...], wc_ref[...], preferred_element_type=F32)
         + jnp.dot(a_ref[...], wa_ref[...], preferred_element_type=F32))
    o_ref[...] = y
    on_ref[...] = _rms(y, g_ref[...], RMS_EPS).astype(BF16)


def _outproj(x, conv_out, att, w_out, next_norm_g):
    m, d = x.shape
    tm = PROJ_TM
    return pl.pallas_call(
        _outproj_kernel,
        grid=(m // tm,),
        in_specs=[
            pl.BlockSpec((tm, d), lambda i: (i, 0)),
            pl.BlockSpec((tm, CONV_CH), lambda i: (i, 0)),
            pl.BlockSpec((tm, ATTN_V), lambda i: (i, 0)),
            pl.BlockSpec((CONV_CH, d), lambda i: (0, 0)),
            pl.BlockSpec((ATTN_V, d), lambda i: (1, 0)),
            pl.BlockSpec((1, d), lambda i: (0, 0)),
        ],
        out_specs=[pl.BlockSpec((tm, d), lambda i: (i, 0)), pl.BlockSpec((tm, d), lambda i: (i, 0))],
        out_shape=[jax.ShapeDtypeStruct((m, d), F32), jax.ShapeDtypeStruct((m, d), BF16)],
        compiler_params=pltpu.CompilerParams(
            dimension_semantics=("arbitrary",),
            vmem_limit_bytes=VMEM_LIMIT_BYTES),
        name="outproj",
    )(x, conv_out, att, w_out, w_out, next_norm_g.reshape(1, d))


def _trunk(x, w, bias):
    b, s, d = x.shape
    m = b * s
    x0 = x.reshape(m, d)
    x1 = _ffn(x0, w["ffn1_norm"], w["ffn1_w_in"], w["ffn1_w_out"])
    u, qkv = _inproj(x1, w["mix_norm"], w["w_in"])
    conv_out = _conv(u.reshape(b, s, CONV_CH), w["conv_w"], w["conv_b"], w["conv_ln_g"], w["conv_ln_b"])
    att = _attn(qkv.reshape(b, s, -1), *bias, w["lambda_q1"], w["lambda_k1"],
                w["lambda_q2"], w["lambda_k2"], w["subln_g"])
    x2, xn2 = _outproj(x1, conv_out.reshape(m, CONV_CH), att.reshape(m, ATTN_V), w["w_out"], w["ffn2_norm"])
    y = _ffn(x2, w["ffn2_norm"], w["ffn2_w_in"], w["ffn2_w_out"], final_g=w["final_norm"], xn=xn2)
    return y.reshape(b, s, d)


def kernel(x_prompt, x_sample, rel_bias, ffn1_norm, ffn1_w_in, ffn1_w_out, mix_norm, w_in, conv_w, conv_b, conv_ln_g, conv_ln_b, lambda_q1, lambda_k1, lambda_q2, lambda_k2, subln_g, w_out, ffn2_norm, ffn2_w_in, ffn2_w_out, final_norm):
    assert ffn1_norm.shape[0] == 1, "single-layer trunk"
    w = dict(
        ffn1_norm=ffn1_norm[0], ffn1_w_in=ffn1_w_in[0].astype(BF16), ffn1_w_out=ffn1_w_out[0].astype(BF16),
        mix_norm=mix_norm[0], w_in=w_in[0].astype(BF16),
        conv_w=conv_w[0], conv_b=conv_b[0], conv_ln_g=conv_ln_g[0], conv_ln_b=conv_ln_b[0],
        lambda_q1=lambda_q1[0], lambda_k1=lambda_k1[0], lambda_q2=lambda_q2[0], lambda_k2=lambda_k2[0],
        subln_g=subln_g[0], w_out=w_out[0].astype(BF16),
        ffn2_norm=ffn2_norm[0], ffn2_w_in=ffn2_w_in[0].astype(BF16), ffn2_w_out=ffn2_w_out[0].astype(BF16),
        final_norm=final_norm,
    )
    bias = _bias_tiles(rel_bias, ATTN_T)
    return (_trunk(x_prompt, w, bias), _trunk(x_sample, w, bias))
```
